```python
import math
import jax, jax.numpy as jnp
from jax import lax
import numpy as np

D_MODEL = 1024
BATCH = 16
SEQ = 2048
DEPTH = 2

N_MIXERS = 2
N_ATTN_LAYERS = (DEPTH + 1) // 2
N_MLSTM_LAYERS = DEPTH // 2

DA_HEADS = 8
DA_HEAD_DIM = 64
DA_V_DIM = 2 * DA_HEAD_DIM
DA_PROJ = 3 * DA_HEADS * 2 * DA_HEAD_DIM
Q_BLOCK = 128

REL_BUCKETS = 32
REL_MAX_DIST = 128

ML_HEADS = 4
ML_V_DIM = D_MODEL // ML_HEADS
ML_QK_DIM = ML_V_DIM // 2
ML_CHUNK = 64
CONV_WIDTH = 4
ML_QK_WIDTH = 2 * ML_HEADS * ML_QK_DIM
ML_V_WIDTH = ML_HEADS * ML_V_DIM
ML_PROJ = ML_QK_WIDTH + 2 * ML_V_WIDTH + 2 * ML_HEADS

D_FF = 4 * D_MODEL

EPS = 1e-6

kernel_name = "hybrid_diffattn_mlstm_sqrelu"


def rmsnorm(x, g):
    xf = x.astype(jnp.float32)
    y = xf * lax.rsqrt(jnp.mean(xf * xf, axis=-1, keepdims=True) + EPS)
    return (y * g.astype(jnp.float32)).astype(x.dtype)


def t5_causal_bucket(dist):
    max_exact = REL_BUCKETS // 2
    d = jnp.maximum(dist, 1).astype(jnp.float32)
    large = max_exact + (jnp.log(d / max_exact) / math.log(REL_MAX_DIST / max_exact)
                         * (REL_BUCKETS - max_exact)).astype(jnp.int32)
    large = jnp.minimum(large, REL_BUCKETS - 1)
    return jnp.where(dist < max_exact, dist, large)


def diff_attention(h, w_in, lq1, lk1, lq2, lk2, subln_w, w_out, rel_table, lambda_init):
    B, S, _ = h.shape
    qkv = h @ w_in
    q, k, v = jnp.split(qkv, 3, axis=-1)
    q = q.reshape(B, S, DA_HEADS, 2, DA_HEAD_DIM)
    k = k.reshape(B, S, DA_HEADS, 2, DA_HEAD_DIM)
    v = v.reshape(B, S, DA_HEADS, DA_V_DIM)
    lam = (jnp.exp(jnp.sum(lq1.astype(jnp.float32) * lk1.astype(jnp.float32)))
           - jnp.exp(jnp.sum(lq2.astype(jnp.float32) * lk2.astype(jnp.float32)))
           + lambda_init)
    scale = DA_HEAD_DIM ** -0.5
    outs = []
    for blk in range(S // Q_BLOCK):
        q0 = blk * Q_BLOCK
        kend = q0 + Q_BLOCK
        qb = q[:, q0:kend]
        kb = k[:, :kend]
        vb = v[:, :kend]
        dist = (q0 + jnp.arange(Q_BLOCK))[:, None] - jnp.arange(kend)[None, :]
        bias = rel_table[t5_causal_bucket(jnp.maximum(dist, 0))]
        bias = jnp.transpose(bias, (2, 0, 1)).astype(jnp.float32)
        logits = jnp.einsum('bqhmd,bkhmd->bmhqk', qb, kb).astype(jnp.float32) * scale + bias
        logits = jnp.where(dist >= 0, logits, -jnp.inf)
        p = jax.nn.softmax(logits, axis=-1)
        attn = p[:, 0] - lam * p[:, 1]
        outs.append(jnp.einsum('bhqk,bkhd->bqhd', attn.astype(vb.dtype), vb))
    o = jnp.concatenate(outs, axis=1)
    o = rmsnorm(o, subln_w) * (1.0 - lambda_init)
    return o.reshape(B, S, DA_HEADS * DA_V_DIM) @ w_out


def causal_depthwise_conv(x, w, b):
    C = x.shape[-1]
    y = lax.conv_general_dilated(x, w[:, None, :].astype(x.dtype), window_strides=(1,),
                                 padding=[(CONV_WIDTH - 1, 0)],
                                 dimension_numbers=('NWC', 'WIO', 'NWC'),
                                 feature_group_count=C)
    return y + b.astype(x.dtype)


def mlstm_chunkwise(q, k, v, i_pre, log_f):
    B, S, H, dk = q.shape
    dv = v.shape[-1]
    L = ML_CHUNK
    NC = S // L
    f32 = jnp.float32
    q = (q.astype(f32) * dk ** -0.5).reshape(B, NC, L, H, dk).transpose(0, 3, 1, 2, 4)
    k = k.astype(f32).reshape(B, NC, L, H, dk).transpose(0, 3, 1, 2, 4)
    v = v.astype(f32).reshape(B, NC, L, H, dv).transpose(0, 3, 1, 2, 4)
    ig = i_pre.astype(f32).reshape(B, NC, L, H).transpose(0, 3, 1, 2)
    lf = log_f.astype(f32).reshape(B, NC, L, H).transpose(0, 3, 1, 2)
    b = jnp.cumsum(lf, axis=-1)
    b_last = b[..., -1]

    a = b_last[..., None] - b + ig
    m_loc = jnp.max(a, axis=-1)
    wgt = jnp.exp(a - m_loc[..., None])
    C_loc = jnp.einsum('bhcsv,bhcsk->bhcvk', wgt[..., None] * v, k)
    n_loc = jnp.einsum('bhcs,bhcsk->bhck', wgt, k)

    def step(carry, inp):
        C, n, m = carry
        bl, Cl, nl, ml = inp
        m_new = jnp.maximum(bl + m, ml)
        decay = jnp.exp(bl + m - m_new)
        gain = jnp.exp(ml - m_new)
        C_new = decay[..., None, None] * C + gain[..., None, None] * Cl
        n_new = decay[..., None] * n + gain[..., None] * nl
        return (C_new, n_new, m_new), (C, n, m)

    init = (jnp.zeros((B, H, dv, dk), f32), jnp.zeros((B, H, dk), f32), jnp.zeros((B, H), f32))
    xs = (jnp.moveaxis(b_last, 2, 0), jnp.moveaxis(C_loc, 2, 0),
          jnp.moveaxis(n_loc, 2, 0), jnp.moveaxis(m_loc, 2, 0))
    _, (C_prev, n_prev, m_prev) = lax.scan(step, init, xs)
    C_prev = jnp.moveaxis(C_prev, 0, 2)
    n_prev = jnp.moveaxis(n_prev, 0, 2)
    m_prev = jnp.moveaxis(m_prev, 0, 2)

    causal = jnp.tril(jnp.ones((L, L), dtype=bool))
    D = b[..., :, None] - b[..., None, :] + ig[..., None, :]
    D = jnp.where(causal, D, -jnp.inf)
    m_inter = b + m_prev[..., None]
    m_j = jnp.maximum(m_inter, jnp.max(D, axis=-1))
    Sc = jnp.einsum('bhcjk,bhcsk->bhcjs', q, k) * jnp.exp(D - m_j[..., None])
    inter = jnp.exp(m_inter - m_j)
    num = (jnp.einsum('bhcjs,bhcsv->bhcjv', Sc, v)
           + inter[..., None] * jnp.einsum('bhcjk,bhcvk->bhcjv', q, C_prev))
    den = jnp.sum(Sc, axis=-1) + inter * jnp.einsum('bhcjk,bhck->bhcj', q, n_prev)
    h = num / jnp.maximum(jnp.abs(den), jnp.exp(-m_j))[..., None]
    return h.transpose(0, 2, 3, 1, 4).reshape(B, S, H, dv)


def mlstm_mixer(h, w_in, b_i, b_f, conv_w, conv_b, head_norm_w, w_out):
    B, S, _ = h.shape
    proj = h @ w_in
    qk = proj[..., :ML_QK_WIDTH]
    v = proj[..., ML_QK_WIDTH:ML_QK_WIDTH + ML_V_WIDTH]
    o_pre = proj[..., ML_QK_WIDTH + ML_V_WIDTH:ML_QK_WIDTH + 2 * ML_V_WIDTH]
    gates = proj[..., ML_QK_WIDTH + 2 * ML_V_WIDTH:].astype(jnp.float32)
    qk = jax.nn.silu(causal_depthwise_conv(qk, conv_w, conv_b))
    q = qk[..., :ML_QK_WIDTH // 2].reshape(B, S, ML_HEADS, ML_QK_DIM)
    k = qk[..., ML_QK_WIDTH // 2:].reshape(B, S, ML_HEADS, ML_QK_DIM)
    v = v.reshape(B, S, ML_HEADS, ML_V_DIM)
    i_pre = gates[..., :ML_HEADS] + b_i.astype(jnp.float32)
    log_f = jax.nn.log_sigmoid(gates[..., ML_HEADS:] + b_f.astype(jnp.float32))
    ht = mlstm_chunkwise(q, k, v, i_pre, log_f)
    ht = rmsnorm(ht, head_norm_w.reshape(ML_HEADS, ML_V_DIM))
    out = jax.nn.sigmoid(o_pre.astype(jnp.float32)) * ht.reshape(B, S, ML_V_WIDTH)
    return out.astype(h.dtype) @ w_out


def sqrelu_mlp(h, w1, w2):
    return jnp.square(jax.nn.relu(h @ w1)) @ w2


def setup_inputs(seed: int = 0) -> dict:
    key = jax.random.key(seed)
    ks = jax.random.split(key, 24)
    f32 = jnp.float32
    nA, nM = N_ATTN_LAYERS, N_MLSTM_LAYERS

    def nrm(k, shape, scale):
        return jax.random.normal(k, shape, f32) * scale

    def gain(k, shape):
        return 1.0 + 0.05 * jax.random.normal(k, shape, f32)

    return {
        "x": jax.random.normal(ks[0], (BATCH, SEQ, D_MODEL), f32),
        "rel_bias": nrm(ks[1], (REL_BUCKETS, DA_HEADS), 0.5),
        "attn_norm": gain(ks[2], (nA, D_MODEL)),
        "attn_w_in": nrm(ks[3], (nA, D_MODEL, DA_PROJ), D_MODEL ** -0.5),
        "attn_lambda_q1": nrm(ks[4], (nA, DA_HEAD_DIM), 0.1),
        "attn_lambda_k1": nrm(ks[5], (nA, DA_HEAD_DIM), 0.1),
        "attn_lambda_q2": nrm(ks[6], (nA, DA_HEAD_DIM), 0.1),
        "attn_lambda_k2": nrm(ks[7], (nA, DA_HEAD_DIM), 0.1),
        "attn_subln": gain(ks[8], (nA, DA_V_DIM)),
        "attn_w_out": nrm(ks[9], (nA, DA_HEADS * DA_V_DIM, D_MODEL), (DA_HEADS * DA_V_DIM) ** -0.5),
        "mlstm_norm": gain(ks[10], (nM, D_MODEL)),
        "mlstm_w_in": nrm(ks[11], (nM, D_MODEL, ML_PROJ), D_MODEL ** -0.5),
        "mlstm_b_i": nrm(ks[12], (nM, ML_HEADS), 0.1),
        "mlstm_b_f": jnp.broadcast_to(jnp.linspace(3.0, 6.0, ML_HEADS, dtype=f32), (nM, ML_HEADS))
                     + nrm(ks[13], (nM, ML_HEADS), 0.01),
        "mlstm_conv_w": nrm(ks[14], (nM, CONV_WIDTH, ML_QK_WIDTH), CONV_WIDTH ** -0.5),
        "mlstm_conv_b": nrm(ks[15], (nM, ML_QK_WIDTH), 0.02),
        "mlstm_head_norm": gain(ks[16], (nM, ML_V_WIDTH)),
        "mlstm_w_out": nrm(ks[17], (nM, ML_V_WIDTH, D_MODEL), ML_V_WIDTH ** -0.5),
        "mlp_norm": gain(ks[18], (DEPTH, D_MODEL)),
        "mlp_w1": nrm(ks[19], (DEPTH, D_MODEL, D_FF), D_MODEL ** -0.5),
        "mlp_w2": nrm(ks[20], (DEPTH, D_FF, D_MODEL), D_FF ** -0.5),
        "final_norm": gain(ks[21], (D_MODEL,)),
    }


def reference(x, rel_bias, attn_norm, attn_w_in, attn_lambda_q1, attn_lambda_k1,
              attn_lambda_q2, attn_lambda_k2, attn_subln, attn_w_out,
              mlstm_norm, mlstm_w_in, mlstm_b_i, mlstm_b_f, mlstm_conv_w, mlstm_conv_b,
              mlstm_head_norm, mlstm_w_out, mlp_norm, mlp_w1, mlp_w2, final_norm):
    h = x
    for layer in range(DEPTH):
        j = layer // N_MIXERS
        if layer % N_MIXERS == 0:
            lambda_init = 0.8 - 0.6 * math.exp(-0.3 * layer)
            mix = diff_attention(rmsnorm(h, attn_norm[j]), attn_w_in[j],
                                 attn_lambda_q1[j], attn_lambda_k1[j],
                                 attn_lambda_q2[j], attn_lambda_k2[j],
                                 attn_subln[j], attn_w_out[j], rel_bias, lambda_init)
        else:
            mix = mlstm_mixer(rmsnorm(h, mlstm_norm[j]), mlstm_w_in[j], mlstm_b_i[j],
                              mlstm_b_f[j], mlstm_conv_w[j], mlstm_conv_b[j],
                              mlstm_head_norm[j], mlstm_w_out[j])
        h = h + mix.astype(h.dtype)
        h = h + sqrelu_mlp(rmsnorm(h, mlp_norm[layer]), mlp_w1[layer], mlp_w2[layer]).astype(h.dtype)
    return rmsnorm(h, final_norm)
```

```python
import functools
import math

import numpy as np
import jax
import jax.numpy as jnp
from jax import lax
from jax.experimental import pallas as pl
from jax.experimental.pallas import tpu as pltpu

F32 = jnp.float32
BF16 = jnp.bfloat16

EPS = 1e-6
NEG = -1e30

DA_HEADS = 8
DA_HEAD_DIM = 64
DA_V_DIM = 2 * DA_HEAD_DIM
REL_BUCKETS = 32
REL_MAX_DIST = 128

ML_HEADS = 4
ML_QK_DIM = 128
ML_V_DIM = 256
CONV_WIDTH = 4
ML_GATE_PAD = 128

LANES = 128
BF16_SUBLANES = 16
VMEM_LIMIT = 56 * 1024 * 1024

ROW_TILE = 512
ATTN_TILE = 256
ML_CHUNK = 256
HALO = BF16_SUBLANES


def _params(*sem):
    return pltpu.CompilerParams(dimension_semantics=sem, vmem_limit_bytes=VMEM_LIMIT)


def _const_spec(shape):
    nd = len(shape)
    return pl.BlockSpec(shape, lambda *_: (0,) * nd)


def _rmsnorm_rows(x, g):
    return x * lax.rsqrt(jnp.mean(x * x, axis=-1, keepdims=True) + EPS) * g


def _attn_proj_kernel(x_ref, g_ref, w_ref, o_ref, *, n_chunk, q_width, q_scale):
    xn = _rmsnorm_rows(x_ref[...], g_ref[...]).astype(BF16)
    n_out = o_ref.shape[-1]
    for c in range(n_out // n_chunk):
        sl = slice(c * n_chunk, (c + 1) * n_chunk)
        y = jnp.dot(xn, w_ref[:, sl], preferred_element_type=F32)
        if (c + 1) * n_chunk <= q_width:
            y = y * q_scale
        o_ref[:, sl] = y.astype(BF16)


def _attn_proj(x2, g, w_bf16):
    t, d = x2.shape
    n = w_bf16.shape[1]
    q_width = n // 3
    n_chunk = 512
    assert t % ROW_TILE == 0 and q_width % n_chunk == 0
    kern = functools.partial(_attn_proj_kernel, n_chunk=n_chunk, q_width=q_width,
                             q_scale=DA_HEAD_DIM ** -0.5)
    return pl.pallas_call(
        kern,
        grid=(t // ROW_TILE,),
        in_specs=[pl.BlockSpec((ROW_TILE, d), lambda i: (i, 0)),
                  _const_spec((1, d)),
                  _const_spec((d, n))],
        out_specs=pl.BlockSpec((ROW_TILE, n), lambda i: (i, 0)),
        out_shape=jax.ShapeDtypeStruct((t, n), BF16),
        compiler_params=_params("parallel"),
        name="attn_proj",
    )(x2, g.reshape(1, d), w_bf16)


def _bucket_band(tile):
    i = np.arange(tile)[:, None]
    r = np.arange(2 * tile)[None, :]
    dist = i + tile - r
    max_exact = REL_BUCKETS // 2
    d = np.maximum(dist, 1).astype(np.float32)
    large = max_exact + (np.log(d / np.float32(max_exact)) / np.float32(math.log(REL_MAX_DIST / max_exact))
                         * np.float32(REL_BUCKETS - max_exact)).astype(np.int32)
    large = np.minimum(large, REL_BUCKETS - 1)
    bucket = np.where(dist < max_exact, dist, large)
    return np.where(dist < 0, -1, bucket).astype(np.int32)


def _bias_band_kernel(rel_ref, bucket_ref, o_ref):
    h = pl.program_id(0)
    bucket = bucket_ref[...]
    acc = jnp.full(bucket.shape, NEG, F32)
    for b in range(REL_BUCKETS):
        acc = jnp.where(bucket == b, rel_ref[b, h], acc)
    o_ref[0] = acc


def _bias_band(rel_bias, tile):
    bucket = jnp.asarray(_bucket_band(tile))
    return pl.pallas_call(
        _bias_band_kernel,
        grid=(DA_HEADS,),
        in_specs=[pl.BlockSpec(memory_space=pltpu.SMEM),
                  _const_spec((tile, 2 * tile))],
        out_specs=pl.BlockSpec((1, tile, 2 * tile), lambda h: (h, 0, 0)),
        out_shape=jax.ShapeDtypeStruct((DA_HEADS, tile, 2 * tile), F32),
        compiler_params=_params("arbitrary"),
        name="bias_band",
    )(rel_bias, bucket)


def _diff_attn_kernel(rel_ref, lam_ref, q_ref, k_ref, v_ref, band_ref, sub_ref, o_ref,
                      *, tile, lambda_init):
    i = pl.program_id(1)
    lp = lam_ref[...]
    lam = (jnp.exp(jnp.sum(lp[0:1] * lp[1:2], axis=-1, keepdims=True))
           - jnp.exp(jnp.sum(lp[2:3] * lp[3:4], axis=-1, keepdims=True)) + lambda_init)
    lane = lax.broadcasted_iota(jnp.int32, (tile, DA_V_DIM), 1)
    first_map = (lane < DA_HEAD_DIM).astype(F32)
    sub_w = sub_ref[...]

    def step(qz, k_blk, v_blk, bias, carry):
        m, l, acc = carry
        s = lax.dot_general(qz, k_blk, (((1,), (1,)), ((), ())), preferred_element_type=F32)
        s = s + bias
        m_new = jnp.maximum(m, jnp.max(s, axis=-1, keepdims=True))
        alpha = jnp.exp(m - m_new)
        p = jnp.exp(s - m_new)
        l = alpha * l + jnp.sum(p, axis=-1, keepdims=True)
        acc = alpha * acc + jnp.dot(p.astype(BF16), v_blk, preferred_element_type=F32)
        return m_new, l, acc

    for h in range(DA_HEADS):
        cols = slice(h * DA_V_DIM, (h + 1) * DA_V_DIM)
        q = q_ref[0, :, cols].astype(F32)
        qz = jnp.concatenate([(q * first_map).astype(BF16),
                              (q * (1.0 - first_map)).astype(BF16)], axis=0)
        far_bias = rel_ref[REL_BUCKETS - 1, h]

        def far_step(j, carry, qz=qz, cols=cols, far_bias=far_bias):
            rows = pl.ds(pl.multiple_of(j * tile, tile), tile)
            return step(qz, k_ref[0, rows, cols], v_ref[0, rows, cols], far_bias, carry)

        carry = (jnp.full((2 * tile, 1), NEG, F32),
                 jnp.zeros((2 * tile, 1), F32),
                 jnp.zeros((2 * tile, DA_V_DIM), F32))
        carry = lax.fori_loop(0, jnp.maximum(i - 1, 0), far_step, carry)

        prev_bias = band_ref[h, :, 0:tile]
        prev_rows = pl.ds(pl.multiple_of(jnp.maximum(i - 1, 0) * tile, tile), tile)
        prev_carry = step(qz, k_ref[0, prev_rows, cols], v_ref[0, prev_rows, cols],
                          jnp.concatenate([prev_bias, prev_bias], axis=0), carry)
        has_prev = i > 0
        carry = tuple(jnp.where(has_prev, a, b) for a, b in zip(prev_carry, carry))

        diag_bias = band_ref[h, :, tile:2 * tile]
        diag_rows = pl.ds(pl.multiple_of(i * tile, tile), tile)
        m, l, acc = step(qz, k_ref[0, diag_rows, cols], v_ref[0, diag_rows, cols],
                         jnp.concatenate([diag_bias, diag_bias], axis=0), carry)

        o = acc / l
        o = o[:tile] - lam * o[tile:]
        o = _rmsnorm_rows(o, sub_w) * (1.0 - lambda_init)
        o_ref[0, :, cols] = o.astype(o_ref.dtype)


def _diff_attn(qkv, band, rel_bias, lam_params, subln, lambda_init):
    b, s, n3 = qkv.shape
    width = n3 // 3
    tile = ATTN_TILE
    assert s % tile == 0 and width == DA_HEADS * DA_V_DIM
    kern = functools.partial(_diff_attn_kernel, tile=tile, lambda_init=lambda_init)
    return pl.pallas_call(
        kern,
        grid=(b, s // tile),
        in_specs=[pl.BlockSpec(memory_space=pltpu.SMEM),
                  _const_spec(lam_params.shape),
                  pl.BlockSpec((1, tile, width), lambda bi, i: (bi, i, 0)),
                  pl.BlockSpec((1, s, width), lambda bi, i: (bi, 0, 1)),
                  pl.BlockSpec((1, s, width), lambda bi, i: (bi, 0, 2)),
                  _const_spec(band.shape),
                  _const_spec((1, DA_V_DIM))],
        out_specs=pl.BlockSpec((1, tile, width), lambda bi, i: (bi, i, 0)),
        out_shape=jax.ShapeDtypeStruct((b, s, width), BF16),
        compiler_params=_params("parallel", "arbitrary"),
        name="diff_attn",
    )(rel_bias, lam_params, qkv, qkv, qkv, band, subln.reshape(1, DA_V_DIM))


def _out_mlp_kernel(a_ref, h_ref, wo_ref, g_ref, w1_ref, w2_ref, gf_ref, o_ref, u_ref,
                    *, ff_chunk, final_norm):
    h1 = h_ref[...] + jnp.dot(a_ref[...], wo_ref[...], preferred_element_type=F32)
    xn = _rmsnorm_rows(h1, g_ref[...]).astype(BF16)
    d_ff = w1_ref.shape[1]
    for c in range(d_ff // ff_chunk):
        sl = slice(c * ff_chunk, (c + 1) * ff_chunk)
        u = jnp.maximum(jnp.dot(xn, w1_ref[:, sl], preferred_element_type=F32), 0.0)
        u_ref[:, sl] = (u * u).astype(BF16)
    h2 = h1 + jnp.dot(u_ref[...], w2_ref[...], preferred_element_type=F32)
    if final_norm:
        h2 = _rmsnorm_rows(h2, gf_ref[...])
    o_ref[...] = h2


def _out_mlp(a2, h2, wo, g, w1, w2, gf, final_norm):
    t, d = h2.shape
    d_ff = w1.shape[1]
    kern = functools.partial(_out_mlp_kernel, ff_chunk=1024, final_norm=final_norm)
    single = pl.Buffered(1)
    return pl.pallas_call(
        kern,
        grid=(t // ROW_TILE,),
        in_specs=[pl.BlockSpec((ROW_TILE, a2.shape[1]), lambda i: (i, 0)),
                  pl.BlockSpec((ROW_TILE, d), lambda i: (i, 0)),
                  pl.BlockSpec(wo.shape, lambda i: (0, 0), pipeline_mode=single),
                  _const_spec((1, d)),
                  pl.BlockSpec(w1.shape, lambda i: (0, 0), pipeline_mode=single),
                  pl.BlockSpec(w2.shape, lambda i: (0, 0), pipeline_mode=single),
                  _const_spec((1, d))],
        out_specs=pl.BlockSpec((ROW_TILE, d), lambda i: (i, 0)),
        out_shape=jax.ShapeDtypeStruct((t, d), F32),
        scratch_shapes=[pltpu.VMEM((ROW_TILE, d_ff), BF16)],
        compiler_params=_params("parallel"),
        name="out_mlp_final" if final_norm else "out_mlp",
    )(a2, h2, wo, g.reshape(1, d), w1, w2, gf.reshape(1, d))


def _log_sigmoid(x):
    return -(jnp.maximum(-x, 0.0) + jnp.log(1.0 + jnp.exp(-jnp.abs(x))))


def _ml_proj_kernel(x_ref, xh_ref, g_ref, wqk_ref, wv_ref, wo_ref, wg_ref, cw_ref, cb_ref, gb_ref,
                    q_ref, k_ref, v_ref, og_ref, gcol_ref, grow_ref, cs_ref,
                    *, tiles_per_seq, q_scale):
    i = pl.program_id(0)
    tm = x_ref.shape[0]
    g = g_ref[...]
    xn = _rmsnorm_rows(x_ref[...], g).astype(BF16)
    xh = _rmsnorm_rows(xh_ref[...], g).astype(BF16)
    halo = jnp.dot(xh, wqk_ref[...], preferred_element_type=F32)
    seq_start = (i % tiles_per_seq) == 0
    cs_ref[0:HALO, :] = jnp.where(seq_start, 0.0, halo)
    cs_ref[HALO:HALO + tm, :] = jnp.dot(xn, wqk_ref[...], preferred_element_type=F32)
    conv = cb_ref[...]
    for j in range(CONV_WIDTH):
        off = HALO - (CONV_WIDTH - 1) + j
        conv = conv + cw_ref[j:j + 1, :] * cs_ref[off:off + tm, :]
    qk = conv * jax.nn.sigmoid(conv)
    half = qk.shape[1] // 2
    q_ref[...] = (qk[:, :half] * q_scale).astype(BF16)
    k_ref[...] = qk[:, half:].astype(BF16)
    v_ref[...] = jnp.dot(xn, wv_ref[...], preferred_element_type=F32).astype(BF16)
    og_ref[...] = jax.nn.sigmoid(jnp.dot(xn, wo_ref[...], preferred_element_type=F32)).astype(BF16)
    gates = jnp.dot(xn, wg_ref[...], preferred_element_type=F32) + gb_ref[...]
    lane = lax.broadcasted_iota(jnp.int32, gates.shape, 1)
    gates = jnp.where(lane < ML_HEADS, gates, _log_sigmoid(gates))
    gcol_ref[...] = gates
    grow_ref[...] = jnp.transpose(gates)[0:grow_ref.shape[0], :]


def _ml_proj(x2, xnorm_g, wqk, wv, wo, wg, conv_w, conv_b, gate_b, seq_len):
    t, d = x2.shape
    tm = ROW_TILE
    assert seq_len % tm == 0 and tm % HALO == 0
    qk_w = wqk.shape[1]
    v_w = wv.shape[1]
    kern = functools.partial(_ml_proj_kernel, tiles_per_seq=seq_len // tm, q_scale=ML_QK_DIM ** -0.5)
    single = pl.Buffered(1)
    halo_blocks = tm // HALO
    out_shapes = (jax.ShapeDtypeStruct((t, qk_w // 2), BF16),
                  jax.ShapeDtypeStruct((t, qk_w // 2), BF16),
                  jax.ShapeDtypeStruct((t, v_w), BF16),
                  jax.ShapeDtypeStruct((t, v_w), BF16),
                  jax.ShapeDtypeStruct((t, ML_GATE_PAD), F32),
                  jax.ShapeDtypeStruct((2 * ML_HEADS, t), F32))
    return pl.pallas_call(
        kern,
        grid=(t // tm,),
        in_specs=[pl.BlockSpec((tm, d), lambda i: (i, 0)),
                  pl.BlockSpec((HALO, d), lambda i: (jnp.maximum(i * halo_blocks - 1, 0), 0)),
                  _const_spec((1, d)),
                  pl.BlockSpec(wqk.shape, lambda i: (0, 0), pipeline_mode=single),
                  pl.BlockSpec(wv.shape, lambda i: (0, 0), pipeline_mode=single),
                  pl.BlockSpec(wo.shape, lambda i: (0, 0), pipeline_mode=single),
                  _const_spec(wg.shape),
                  _const_spec(conv_w.shape),
                  _const_spec((1, qk_w)),
                  _const_spec((1, ML_GATE_PAD))],
        out_specs=(pl.BlockSpec((tm, qk_w // 2), lambda i: (i, 0)),
                   pl.BlockSpec((tm, qk_w // 2), lambda i: (i, 0)),
                   pl.BlockSpec((tm, v_w), lambda i: (i, 0)),
                   pl.BlockSpec((tm, v_w), lambda i: (i, 0)),
                   pl.BlockSpec((tm, ML_GATE_PAD), lambda i: (i, 0)),
                   pl.BlockSpec((2 * ML_HEADS, tm), lambda i: (0, i))),
        out_shape=out_shapes,
        scratch_shapes=[pltpu.VMEM((HALO + tm, qk_w), F32)],
        compiler_params=_params("parallel"),
        name="mlstm_proj",
    )(x2, x2, xnorm_g.reshape(1, d), wqk, wv, wo, wg, conv_w, conv_b.reshape(1, qk_w),
      gate_b.reshape(1, ML_GATE_PAD))


def _mlstm_kernel(q_ref, k_ref, v_ref, og_ref, gcol_ref, grow_ref, hn_ref, o_ref,
                  c_ref, n_ref, m_ref):
    @pl.when(pl.program_id(1) == 0)
    def _():
        c_ref[...] = jnp.zeros_like(c_ref)
        n_ref[...] = jnp.zeros_like(n_ref)
        m_ref[...] = jnp.zeros_like(m_ref)

    L = q_ref.shape[0]
    row = lax.broadcasted_iota(jnp.int32, (L, L), 0)
    col = lax.broadcasted_iota(jnp.int32, (L, L), 1)
    causal = col <= row
    gcol = gcol_ref[...]
    grow = grow_ref[...]
    for h in range(ML_HEADS):
        qh = q_ref[:, h * ML_QK_DIM:(h + 1) * ML_QK_DIM]
        kh = k_ref[:, h * ML_QK_DIM:(h + 1) * ML_QK_DIM]
        vcols = slice(h * ML_V_DIM, (h + 1) * ML_V_DIM)
        vh = v_ref[:, vcols]
        ig_r = grow[h:h + 1, :]
        lf_r = grow[ML_HEADS + h:ML_HEADS + h + 1, :]
        ig_c = gcol[:, h:h + 1]
        lf_c = gcol[:, ML_HEADS + h:ML_HEADS + h + 1]
        m_prev = m_ref[h]
        n_prev = n_ref[h]
        c_prev = c_ref[h]

        b_c = jnp.sum(jnp.where(causal, lf_r, 0.0), axis=1, keepdims=True)
        b_r = jnp.sum(jnp.where(causal, 0.0, lf_c), axis=0, keepdims=True) + lf_r
        b_last = jnp.sum(lf_r, axis=1, keepdims=True)

        d_mat = jnp.where(causal, b_c - b_r + ig_r, NEG)
        m_inter = b_c + m_prev
        m_j = jnp.maximum(m_inter, jnp.max(d_mat, axis=1, keepdims=True))
        s_qk = lax.dot_general(qh, kh, (((1,), (1,)), ((), ())), preferred_element_type=F32)
        sc = s_qk * jnp.exp(d_mat - m_j)
        inter = jnp.exp(m_inter - m_j)
        num = (jnp.dot(sc.astype(BF16), vh, preferred_element_type=F32)
               + inter * jnp.dot(qh, c_prev.astype(BF16), preferred_element_type=F32))
        den = (jnp.sum(sc, axis=1, keepdims=True)
               + inter * jnp.sum(qh.astype(F32) * n_prev, axis=1, keepdims=True))
        hcore = num / jnp.maximum(jnp.abs(den), jnp.exp(-m_j))
        y = _rmsnorm_rows(hcore, hn_ref[:, vcols])
        o_ref[:, vcols] = (og_ref[:, vcols].astype(F32) * y).astype(o_ref.dtype)

        a_c = b_last - b_c + ig_c
        m_loc = jnp.max(a_c, axis=0, keepdims=True)
        m_new = jnp.maximum(b_last + m_prev, m_loc)
        w_c = jnp.exp(a_c - m_new)
        decay = jnp.exp(b_last + m_prev - m_new)
        wv = (w_c * vh.astype(F32)).astype(BF16)
        c_ref[h] = decay * c_prev + lax.dot_general(
            kh, wv, (((0,), (0,)), ((), ())), preferred_element_type=F32)
        n_ref[h] = decay * n_prev + jnp.sum(w_c * kh.astype(F32), axis=0, keepdims=True)
        m_ref[h] = m_new


def _mlstm(q, k, v, og, gcol, grow, head_norm, batch, seq_len):
    t = q.shape[0]
    L = ML_CHUNK
    assert seq_len % L == 0
    nc = seq_len // L
    v_w = v.shape[1]
    row_map = lambda b, c: (b * nc + c, 0)
    return pl.pallas_call(
        _mlstm_kernel,
        grid=(batch, nc),
        in_specs=[pl.BlockSpec((L, q.shape[1]), row_map),
                  pl.BlockSpec((L, k.shape[1]), row_map),
                  pl.BlockSpec((L, v_w), row_map),
                  pl.BlockSpec((L, v_w), row_map),
                  pl.BlockSpec((L, ML_GATE_PAD), row_map),
                  pl.BlockSpec((2 * ML_HEADS, L), lambda b, c: (0, b * nc + c)),
                  _const_spec((1, v_w))],
        out_specs=pl.BlockSpec((L, v_w), row_map),
        out_shape=jax.ShapeDtypeStruct((t, v_w), BF16),
        scratch_shapes=[pltpu.VMEM((ML_HEADS, ML_QK_DIM, ML_V_DIM), F32),
                        pltpu.VMEM((ML_HEADS, 1, ML_QK_DIM), F32),
                        pltpu.VMEM((ML_HEADS, 1, 1), F32)],
        compiler_params=_params("parallel", "arbitrary"),
        name="mlstm_chunk",
    )(q, k, v, og, gcol, grow, head_norm.reshape(1, v_w))


def kernel(x, rel_bias, attn_norm, attn_w_in, attn_lambda_q1, attn_lambda_k1, attn_lambda_q2,
           attn_lambda_k2, attn_subln, attn_w_out, mlstm_norm, mlstm_w_in, mlstm_b_i, mlstm_b_f,
           mlstm_conv_w, mlstm_conv_b, mlstm_head_norm, mlstm_w_out, mlp_norm, mlp_w1, mlp_w2,
           final_norm):
    batch, seq_len, d = x.shape
    t = batch * seq_len
    h = x.reshape(t, d)

    lambda_init = 0.8 - 0.6 * math.exp(-0.3 * 0)
    qkv = _attn_proj(h, attn_norm[0], attn_w_in[0].astype(BF16))
    band = _bias_band(rel_bias, ATTN_TILE)
    lam_params = jnp.stack([attn_lambda_q1[0], attn_lambda_k1[0],
                            attn_lambda_q2[0], attn_lambda_k2[0]]).astype(F32)
    attn = _diff_attn(qkv.reshape(batch, seq_len, -1), band, rel_bias, lam_params,
                      attn_subln[0], lambda_init)
    h = _out_mlp(attn.reshape(t, -1), h, attn_w_out[0].astype(BF16), mlp_norm[0],
                 mlp_w1[0].astype(BF16), mlp_w2[0].astype(BF16), final_norm, final_norm=False)

    qk_w = 2 * ML_HEADS * ML_QK_DIM
    v_w = ML_HEADS * ML_V_DIM
    w_in = mlstm_w_in[0]
    wqk = w_in[:, :qk_w].astype(BF16)
    wv = w_in[:, qk_w:qk_w + v_w].astype(BF16)
    wo = w_in[:, qk_w + v_w:qk_w + 2 * v_w].astype(BF16)
    n_gates = 2 * ML_HEADS
    wg = jnp.pad(w_in[:, qk_w + 2 * v_w:], ((0, 0), (0, ML_GATE_PAD - n_gates))).astype(BF16)
    gate_b = jnp.pad(jnp.concatenate([mlstm_b_i[0], mlstm_b_f[0]]).astype(F32),
                     (0, ML_GATE_PAD - n_gates))
    q, k, v, og, gcol, grow = _ml_proj(h, mlstm_norm[0], wqk, wv, wo, wg, mlstm_conv_w[0],
                                       mlstm_conv_b[0], gate_b, seq_len)
    mix = _mlstm(q, k, v, og, gcol, grow, mlstm_head_norm[0], batch, seq_len)
    out = _out_mlp(mix, h, mlstm_w_out[0].astype(BF16), mlp_norm[1],
                   mlp_w1[1].astype(BF16), mlp_w2[1].astype(BF16), final_norm, final_norm=True)
    return out.reshape(batch, seq_len, d)
```

```python
import functools
import math

import numpy as np
import jax
import jax.numpy as jnp
from jax import lax
from jax.experimental import pallas as pl
from jax.experimental.pallas import tpu as pltpu

F32 = jnp.float32
BF16 = jnp.bfloat16

EPS = 1e-6
NEG = -1e30

DA_HEADS = 8
DA_HEAD_DIM = 64
DA_V_DIM = 2 * DA_HEAD_DIM
REL_BUCKETS = 32
REL_MAX_DIST = 128

ML_HEADS = 4
ML_QK_DIM = 128
ML_V_DIM = 256
CONV_WIDTH = 4
ML_GATE_PAD = 128

LANES = 128
BF16_SUBLANES = 16
VMEM_LIMIT = 56 * 1024 * 1024

ROW_TILE = 512
ATTN_TILE = 256
ML_CHUNK = 256
HALO = BF16_SUBLANES


def _params(*sem):
    return pltpu.CompilerParams(dimension_semantics=sem, vmem_limit_bytes=VMEM_LIMIT)


def _const_spec(shape):
    nd = len(shape)
    return pl.BlockSpec(shape, lambda *_: (0,) * nd)


def _rmsnorm_rows(x, g):
    return x * lax.rsqrt(jnp.mean(x * x, axis=-1, keepdims=True) + EPS) * g


def _attn_proj_kernel(x_ref, g_ref, w_ref, o_ref, *, n_chunk, q_width, q_scale):
    xn = _rmsnorm_rows(x_ref[...], g_ref[...]).astype(BF16)
    n_out = o_ref.shape[-1]
    for c in range(n_out // n_chunk):
        sl = slice(c * n_chunk, (c + 1) * n_chunk)
        y = jnp.dot(xn, w_ref[:, sl], preferred_element_type=F32)
        if (c + 1) * n_chunk <= q_width:
            y = y * q_scale
        o_ref[:, sl] = y.astype(BF16)


def _attn_proj(x2, g, w_bf16):
    t, d = x2.shape
    n = w_bf16.shape[1]
    q_width = n // 3
    n_chunk = 512
    assert t % ROW_TILE == 0 and q_width % n_chunk == 0
    kern = functools.partial(_attn_proj_kernel, n_chunk=n_chunk, q_width=q_width,
                             q_scale=DA_HEAD_DIM ** -0.5)
    return pl.pallas_call(
        kern,
        grid=(t // ROW_TILE,),
        in_specs=[pl.BlockSpec((ROW_TILE, d), lambda i: (i, 0)),
                  _const_spec((1, d)),
                  _const_spec((d, n))],
        out_specs=pl.BlockSpec((ROW_TILE, n), lambda i: (i, 0)),
        out_shape=jax.ShapeDtypeStruct((t, n), BF16),
        compiler_params=_params("parallel"),
        name="attn_proj",
    )(x2, g.reshape(1, d), w_bf16)


def _bucket_band(tile):
    r = np.arange(2 * tile)[:, None]
    i = np.arange(tile)[None, :]
    dist = i + tile - r
    max_exact = REL_BUCKETS // 2
    d = np.maximum(dist, 1).astype(np.float32)
    large = max_exact + (np.log(d / np.float32(max_exact)) / np.float32(math.log(REL_MAX_DIST / max_exact))
                         * np.float32(REL_BUCKETS - max_exact)).astype(np.int32)
    large = np.minimum(large, REL_BUCKETS - 1)
    bucket = np.where(dist < max_exact, dist, large)
    return np.where(dist < 0, -1, bucket).astype(np.int32)


def _bias_band_kernel(rel_ref, bucket_ref, o_ref):
    h = pl.program_id(0)
    bucket = bucket_ref[...]
    acc = jnp.full(bucket.shape, NEG, F32)
    for b in range(REL_BUCKETS):
        acc = jnp.where(bucket == b, rel_ref[b, h], acc)
    o_ref[0] = acc


def _bias_band(rel_bias, tile):
    bucket = jnp.asarray(_bucket_band(tile))
    return pl.pallas_call(
        _bias_band_kernel,
        grid=(DA_HEADS,),
        in_specs=[pl.BlockSpec(memory_space=pltpu.SMEM),
                  _const_spec((2 * tile, tile))],
        out_specs=pl.BlockSpec((1, 2 * tile, tile), lambda h: (h, 0, 0)),
        out_shape=jax.ShapeDtypeStruct((DA_HEADS, 2 * tile, tile), F32),
        compiler_params=_params("arbitrary"),
        name="bias_band",
    )(rel_bias, bucket)


def _diff_attn_kernel(rel_ref, lam_ref, q_ref, k_ref, v_ref, band_ref, sub_ref, o_ref,
                      qz_ref, m_ref, l_ref, acc_ref, *, tile, lambda_init):
    i = pl.program_id(1)
    head_cols = [slice(h * DA_V_DIM, (h + 1) * DA_V_DIM) for h in range(DA_HEADS)]

    sub = lax.broadcasted_iota(jnp.int32, (DA_V_DIM, tile), 0)
    first_map = (sub < DA_HEAD_DIM).astype(F32)
    for h in range(DA_HEADS):
        q_t = jnp.transpose(q_ref[0, :, head_cols[h]].astype(F32))
        qz_ref[h] = jnp.concatenate([q_t * first_map, q_t * (1.0 - first_map)], axis=1).astype(BF16)
    m_ref[...] = jnp.full(m_ref.shape, NEG, F32)
    l_ref[...] = jnp.zeros(l_ref.shape, F32)
    acc_ref[...] = jnp.zeros(acc_ref.shape, F32)

    def step(h, key_tile, band_rows):
        rows = pl.ds(pl.multiple_of(key_tile * tile, tile), tile)
        k_blk = k_ref[0, rows, head_cols[h]]
        v_blk = v_ref[0, rows, head_cols[h]]
        s_t = jnp.dot(k_blk, qz_ref[h], preferred_element_type=F32)
        m = m_ref[h]
        if band_rows is None:
            far_bias = rel_ref[REL_BUCKETS - 1, h]
            m_new = jnp.maximum(m, jnp.max(s_t, axis=0, keepdims=True) + far_bias)
            p = jnp.exp(s_t - (m_new - far_bias))
        else:
            bias = band_ref[h, band_rows, :]
            s_t = s_t + jnp.concatenate([bias, bias], axis=1)
            m_new = jnp.maximum(m, jnp.max(s_t, axis=0, keepdims=True))
            p = jnp.exp(s_t - m_new)
        alpha = jnp.exp(m - m_new)
        l_ref[h] = alpha * l_ref[h] + jnp.sum(p, axis=0, keepdims=True)
        pv = lax.dot_general(v_blk, p.astype(BF16), (((0,), (0,)), ((), ())),
                             preferred_element_type=F32)
        acc_ref[h] = alpha * acc_ref[h] + pv
        m_ref[h] = m_new

    def far_body(j, carry):
        for h in range(DA_HEADS):
            step(h, j, None)
        return carry

    lax.fori_loop(0, jnp.maximum(i - 1, 0), far_body, 0)

    @pl.when(i > 0)
    def _():
        for h in range(DA_HEADS):
            step(h, i - 1, slice(0, tile))

    for h in range(DA_HEADS):
        step(h, i, slice(tile, 2 * tile))

    lp = lam_ref[...]
    lam = (jnp.exp(jnp.sum(lp[0:1] * lp[1:2], axis=-1, keepdims=True))
           - jnp.exp(jnp.sum(lp[2:3] * lp[3:4], axis=-1, keepdims=True)) + lambda_init)
    sub_w = sub_ref[...] * (1.0 - lambda_init)
    for h in range(DA_HEADS):
        o = acc_ref[h] * (1.0 / l_ref[h])
        o = o[:, :tile] - lam * o[:, tile:]
        o = o * lax.rsqrt(jnp.mean(o * o, axis=0, keepdims=True) + EPS)
        o_ref[0, :, head_cols[h]] = (jnp.transpose(o) * sub_w).astype(o_ref.dtype)


def _diff_attn(qkv, band, rel_bias, lam_params, subln, lambda_init):
    b, s, n3 = qkv.shape
    width = n3 // 3
    tile = ATTN_TILE
    assert s % tile == 0 and width == DA_HEADS * DA_V_DIM
    kern = functools.partial(_diff_attn_kernel, tile=tile, lambda_init=lambda_init)
    return pl.pallas_call(
        kern,
        grid=(b, s // tile),
        in_specs=[pl.BlockSpec(memory_space=pltpu.SMEM),
                  _const_spec(lam_params.shape),
                  pl.BlockSpec((1, tile, width), lambda bi, i: (bi, i, 0)),
                  pl.BlockSpec((1, s, width), lambda bi, i: (bi, 0, 1)),
                  pl.BlockSpec((1, s, width), lambda bi, i: (bi, 0, 2)),
                  _const_spec(band.shape),
                  _const_spec((1, DA_V_DIM))],
        out_specs=pl.BlockSpec((1, tile, width), lambda bi, i: (bi, i, 0)),
        out_shape=jax.ShapeDtypeStruct((b, s, width), BF16),
        scratch_shapes=[pltpu.VMEM((DA_HEADS, DA_V_DIM, 2 * tile), BF16),
                        pltpu.VMEM((DA_HEADS, 1, 2 * tile), F32),
                        pltpu.VMEM((DA_HEADS, 1, 2 * tile), F32),
                        pltpu.VMEM((DA_HEADS, DA_V_DIM, 2 * tile), F32)],
        compiler_params=_params("parallel", "arbitrary"),
        name="diff_attn",
    )(rel_bias, lam_params, qkv, qkv, qkv, band, subln.reshape(1, DA_V_DIM))


def _out_mlp_kernel(a_ref, h_ref, wo_ref, g_ref, w1_ref, w2_ref, gf_ref, o_ref, u_ref,
                    *, ff_chunk, final_norm):
    h1 = h_ref[...] + jnp.dot(a_ref[...], wo_ref[...], preferred_element_type=F32)
    xn = _rmsnorm_rows(h1, g_ref[...]).astype(BF16)
    d_ff = w1_ref.shape[1]
    for c in range(d_ff // ff_chunk):
        sl = slice(c * ff_chunk, (c + 1) * ff_chunk)
        u = jnp.maximum(jnp.dot(xn, w1_ref[:, sl], preferred_element_type=F32), 0.0)
        u_ref[:, sl] = (u * u).astype(BF16)
    h2 = h1 + jnp.dot(u_ref[...], w2_ref[...], preferred_element_type=F32)
    if final_norm:
        h2 = _rmsnorm_rows(h2, gf_ref[...])
    o_ref[...] = h2


def _out_mlp(a2, h2, wo, g, w1, w2, gf, final_norm):
    t, d = h2.shape
    d_ff = w1.shape[1]
    kern = functools.partial(_out_mlp_kernel, ff_chunk=1024, final_norm=final_norm)
    single = pl.Buffered(1)
    return pl.pallas_call(
        kern,
        grid=(t // ROW_TILE,),
        in_specs=[pl.BlockSpec((ROW_TILE, a2.shape[1]), lambda i: (i, 0)),
                  pl.BlockSpec((ROW_TILE, d), lambda i: (i, 0)),
                  pl.BlockSpec(wo.shape, lambda i: (0, 0), pipeline_mode=single),
                  _const_spec((1, d)),
                  pl.BlockSpec(w1.shape, lambda i: (0, 0), pipeline_mode=single),
                  pl.BlockSpec(w2.shape, lambda i: (0, 0), pipeline_mode=single),
                  _const_spec((1, d))],
        out_specs=pl.BlockSpec((ROW_TILE, d), lambda i: (i, 0)),
        out_shape=jax.ShapeDtypeStruct((t, d), F32),
        scratch_shapes=[pltpu.VMEM((ROW_TILE, d_ff), BF16)],
        compiler_params=_params("parallel"),
        name="out_mlp_final" if final_norm else "out_mlp",
    )(a2, h2, wo, g.reshape(1, d), w1, w2, gf.reshape(1, d))


def _log_sigmoid(x):
    return -(jnp.maximum(-x, 0.0) + jnp.log(1.0 + jnp.exp(-jnp.abs(x))))


def _ml_proj_kernel(x_ref, xh_ref, g_ref, wqk_ref, wv_ref, wo_ref, wg_ref, cw_ref, cb_ref, gb_ref,
                    q_ref, k_ref, v_ref, og_ref, gcol_ref, grow_ref, cs_ref,
                    *, tiles_per_seq, q_scale):
    i = pl.program_id(0)
    tm = x_ref.shape[0]
    g = g_ref[...]
    xn = _rmsnorm_rows(x_ref[...], g).astype(BF16)
    xh = _rmsnorm_rows(xh_ref[...], g).astype(BF16)
    halo = jnp.dot(xh, wqk_ref[...], preferred_element_type=F32)
    seq_start = (i % tiles_per_seq) == 0
    cs_ref[0:HALO, :] = jnp.where(seq_start, 0.0, halo)
    cs_ref[HALO:HALO + tm, :] = jnp.dot(xn, wqk_ref[...], preferred_element_type=F32)
    conv = cb_ref[...]
    for j in range(CONV_WIDTH):
        off = HALO - (CONV_WIDTH - 1) + j
        conv = conv + cw_ref[j:j + 1, :] * cs_ref[off:off + tm, :]
    qk = conv * jax.nn.sigmoid(conv)
    half = qk.shape[1] // 2
    q_ref[...] = (qk[:, :half] * q_scale).astype(BF16)
    k_ref[...] = qk[:, half:].astype(BF16)
    v_ref[...] = jnp.dot(xn, wv_ref[...], preferred_element_type=F32).astype(BF16)
    og_ref[...] = jax.nn.sigmoid(jnp.dot(xn, wo_ref[...], preferred_element_type=F32)).astype(BF16)
    gates = jnp.dot(xn, wg_ref[...], preferred_element_type=F32) + gb_ref[...]
    lane = lax.broadcasted_iota(jnp.int32, gates.shape, 1)
    gates = jnp.where(lane < ML_HEADS, gates, _log_sigmoid(gates))
    gcol_ref[...] = gates
    grow_ref[...] = jnp.transpose(gates)[0:grow_ref.shape[0], :]


def _ml_proj(x2, xnorm_g, wqk, wv, wo, wg, conv_w, conv_b, gate_b, seq_len):
    t, d = x2.shape
    tm = ROW_TILE
    assert seq_len % tm == 0 and tm % HALO == 0
    qk_w = wqk.shape[1]
    v_w = wv.shape[1]
    kern = functools.partial(_ml_proj_kernel, tiles_per_seq=seq_len // tm, q_scale=ML_QK_DIM ** -0.5)
    single = pl.Buffered(1)
    halo_blocks = tm // HALO
    out_shapes = (jax.ShapeDtypeStruct((t, qk_w // 2), BF16),
                  jax.ShapeDtypeStruct((t, qk_w // 2), BF16),
                  jax.ShapeDtypeStruct((t, v_w), BF16),
                  jax.ShapeDtypeStruct((t, v_w), BF16),
                  jax.ShapeDtypeStruct((t, ML_GATE_PAD), F32),
                  jax.ShapeDtypeStruct((2 * ML_HEADS, t), F32))
    return pl.pallas_call(
        kern,
        grid=(t // tm,),
        in_specs=[pl.BlockSpec((tm, d), lambda i: (i, 0)),
                  pl.BlockSpec((HALO, d), lambda i: (jnp.maximum(i * halo_blocks - 1, 0), 0)),
                  _const_spec((1, d)),
                  pl.BlockSpec(wqk.shape, lambda i: (0, 0), pipeline_mode=single),
                  pl.BlockSpec(wv.shape, lambda i: (0, 0), pipeline_mode=single),
                  pl.BlockSpec(wo.shape, lambda i: (0, 0), pipeline_mode=single),
                  _const_spec(wg.shape),
                  _const_spec(conv_w.shape),
                  _const_spec((1, qk_w)),
                  _const_spec((1, ML_GATE_PAD))],
        out_specs=(pl.BlockSpec((tm, qk_w // 2), lambda i: (i, 0)),
                   pl.BlockSpec((tm, qk_w // 2), lambda i: (i, 0)),
                   pl.BlockSpec((tm, v_w), lambda i: (i, 0)),
                   pl.BlockSpec((tm, v_w), lambda i: (i, 0)),
                   pl.BlockSpec((tm, ML_GATE_PAD), lambda i: (i, 0)),
                   pl.BlockSpec((2 * ML_HEADS, tm), lambda i: (0, i))),
        out_shape=out_shapes,
        scratch_shapes=[pltpu.VMEM((HALO + tm, qk_w), F32)],
        compiler_params=_params("parallel"),
        name="mlstm_proj",
    )(x2, x2, xnorm_g.reshape(1, d), wqk, wv, wo, wg, conv_w, conv_b.reshape(1, qk_w),
      gate_b.reshape(1, ML_GATE_PAD))


def _mlstm_kernel(q_ref, k_ref, v_ref, og_ref, gcol_ref, grow_ref, hn_ref, o_ref,
                  c_ref, n_ref, m_ref):
    @pl.when(pl.program_id(1) == 0)
    def _():
        c_ref[...] = jnp.zeros_like(c_ref)
        n_ref[...] = jnp.zeros_like(n_ref)
        m_ref[...] = jnp.zeros_like(m_ref)

    L = q_ref.shape[0]
    row = lax.broadcasted_iota(jnp.int32, (L, L), 0)
    col = lax.broadcasted_iota(jnp.int32, (L, L), 1)
    causal = col <= row
    gcol = gcol_ref[...]
    grow = grow_ref[...]
    for h in range(ML_HEADS):
        qh = q_ref[:, h * ML_QK_DIM:(h + 1) * ML_QK_DIM]
        kh = k_ref[:, h * ML_QK_DIM:(h + 1) * ML_QK_DIM]
        vcols = slice(h * ML_V_DIM, (h + 1) * ML_V_DIM)
        vh = v_ref[:, vcols]
        ig_r = grow[h:h + 1, :]
        lf_r = grow[ML_HEADS + h:ML_HEADS + h + 1, :]
        ig_c = gcol[:, h:h + 1]
        lf_c = gcol[:, ML_HEADS + h:ML_HEADS + h + 1]
        m_prev = m_ref[h]
        n_prev = n_ref[h]
        c_prev = c_ref[h]

        b_c = jnp.sum(jnp.where(causal, lf_r, 0.0), axis=1, keepdims=True)
        b_r = jnp.sum(jnp.where(causal, 0.0, lf_c), axis=0, keepdims=True) + lf_r
        b_last = jnp.sum(lf_r, axis=1, keepdims=True)

        d_mat = jnp.where(causal, b_c - b_r + ig_r, NEG)
        m_inter = b_c + m_prev
        m_j = jnp.maximum(m_inter, jnp.max(d_mat, axis=1, keepdims=True))
        s_qk = lax.dot_general(qh, kh, (((1,), (1,)), ((), ())), preferred_element_type=F32)
        sc = s_qk * jnp.exp(d_mat - m_j)
        inter = jnp.exp(m_inter - m_j)
        num = (jnp.dot(sc.astype(BF16), vh, preferred_element_type=F32)
               + inter * jnp.dot(qh, c_prev.astype(BF16), preferred_element_type=F32))
        den = (jnp.sum(sc, axis=1, keepdims=True)
               + inter * jnp.sum(qh.astype(F32) * n_prev, axis=1, keepdims=True))
        hcore = num / jnp.maximum(jnp.abs(den), jnp.exp(-m_j))
        y = _rmsnorm_rows(hcore, hn_ref[:, vcols])
        o_ref[:, vcols] = (og_ref[:, vcols].astype(F32) * y).astype(o_ref.dtype)

        a_c = b_last - b_c + ig_c
        m_loc = jnp.max(a_c, axis=0, keepdims=True)
        m_new = jnp.maximum(b_last + m_prev, m_loc)
        w_c = jnp.exp(a_c - m_new)
        decay = jnp.exp(b_last + m_prev - m_new)
        wv = (w_c * vh.astype(F32)).astype(BF16)
        c_ref[h] = decay * c_prev + lax.dot_general(
            kh, wv, (((0,), (0,)), ((), ())), preferred_element_type=F32)
        n_ref[h] = decay * n_prev + jnp.sum(w_c * kh.astype(F32), axis=0, keepdims=True)
        m_ref[h] = m_new


def _mlstm(q, k, v, og, gcol, grow, head_norm, batch, seq_len):
    t = q.shape[0]
    L = ML_CHUNK
    assert seq_len % L == 0
    nc = seq_len // L
    v_w = v.shape[1]
    row_map = lambda b, c: (b * nc + c, 0)
    return pl.pallas_call(
        _mlstm_kernel,
        grid=(batch, nc),
        in_specs=[pl.BlockSpec((L, q.shape[1]), row_map),
                  pl.BlockSpec((L, k.shape[1]), row_map),
                  pl.BlockSpec((L, v_w), row_map),
                  pl.BlockSpec((L, v_w), row_map),
                  pl.BlockSpec((L, ML_GATE_PAD), row_map),
                  pl.BlockSpec((2 * ML_HEADS, L), lambda b, c: (0, b * nc + c)),
                  _const_spec((1, v_w))],
        out_specs=pl.BlockSpec((L, v_w), row_map),
        out_shape=jax.ShapeDtypeStruct((t, v_w), BF16),
        scratch_shapes=[pltpu.VMEM((ML_HEADS, ML_QK_DIM, ML_V_DIM), F32),
                        pltpu.VMEM((ML_HEADS, 1, ML_QK_DIM), F32),
                        pltpu.VMEM((ML_HEADS, 1, 1), F32)],
        compiler_params=_params("parallel", "arbitrary"),
        name="mlstm_chunk",
    )(q, k, v, og, gcol, grow, head_norm.reshape(1, v_w))


def kernel(x, rel_bias, attn_norm, attn_w_in, attn_lambda_q1, attn_lambda_k1, attn_lambda_q2,
           attn_lambda_k2, attn_subln, attn_w_out, mlstm_norm, mlstm_w_in, mlstm_b_i, mlstm_b_f,
           mlstm_conv_w, mlstm_conv_b, mlstm_head_norm, mlstm_w_out, mlp_norm, mlp_w1, mlp_w2,
           final_norm):
    batch, seq_len, d = x.shape
    t = batch * seq_len
    h = x.reshape(t, d)

    lambda_init = 0.8 - 0.6 * math.exp(-0.3 * 0)
    qkv = _attn_proj(h, attn_norm[0], attn_w_in[0].astype(BF16))
    band = _bias_band(rel_bias, ATTN_TILE)
    lam_params = jnp.stack([attn_lambda_q1[0], attn_lambda_k1[0],
                            attn_lambda_q2[0], attn_lambda_k2[0]]).astype(F32)
    attn = _diff_attn(qkv.reshape(batch, seq_len, -1), band, rel_bias, lam_params,
                      attn_subln[0], lambda_init)
    h = _out_mlp(attn.reshape(t, -1), h, attn_w_out[0].astype(BF16), mlp_norm[0],
                 mlp_w1[0].astype(BF16), mlp_w2[0].astype(BF16), final_norm, final_norm=False)

    qk_w = 2 * ML_HEADS * ML_QK_DIM
    v_w = ML_HEADS * ML_V_DIM
    w_in = mlstm_w_in[0]
    wqk = w_in[:, :qk_w].astype(BF16)
    wv = w_in[:, qk_w:qk_w + v_w].astype(BF16)
    wo = w_in[:, qk_w + v_w:qk_w + 2 * v_w].astype(BF16)
    n_gates = 2 * ML_HEADS
    wg = jnp.pad(w_in[:, qk_w + 2 * v_w:], ((0, 0), (0, ML_GATE_PAD - n_gates))).astype(BF16)
    gate_b = jnp.pad(jnp.concatenate([mlstm_b_i[0], mlstm_b_f[0]]).astype(F32),
                     (0, ML_GATE_PAD - n_gates))
    q, k, v, og, gcol, grow = _ml_proj(h, mlstm_norm[0], wqk, wv, wo, wg, mlstm_conv_w[0],
                                       mlstm_conv_b[0], gate_b, seq_len)
    mix = _mlstm(q, k, v, og, gcol, grow, mlstm_head_norm[0], batch, seq_len)
    out = _out_mlp(mix, h, mlstm_w_out[0].astype(BF16), mlp_norm[1],
                   mlp_w1[1].astype(BF16), mlp_w2[1].astype(BF16), final_norm, final_norm=True)
    return out.reshape(batch, seq_len, d)
```

```python
import functools
import math

import numpy as np
import jax
import jax.numpy as jnp
from jax import lax
from jax.experimental import pallas as pl
from jax.experimental.pallas import tpu as pltpu

F32 = jnp.float32
BF16 = jnp.bfloat16

EPS = 1e-6
NEG = -1e30

DA_HEADS = 8
DA_HEAD_DIM = 64
DA_V_DIM = 2 * DA_HEAD_DIM
REL_BUCKETS = 32
REL_MAX_DIST = 128
DA_VT_ROWS = DA_V_DIM + 16
LOG2E = math.log2(math.e)

ML_HEADS = 4
ML_QK_DIM = 128
ML_V_DIM = 256
CONV_WIDTH = 4
ML_GATE_PAD = 128

LANES = 128
BF16_SUBLANES = 16
VMEM_LIMIT = 56 * 1024 * 1024

ROW_TILE = 512
ATTN_TILE = 256
ML_CHUNK = 256
ATTN_PIPE_DEPTH = 4
ATTN_RING = 4
HALO = BF16_SUBLANES


def _params(*sem):
    return pltpu.CompilerParams(dimension_semantics=sem, vmem_limit_bytes=VMEM_LIMIT)


def _const_spec(shape):
    nd = len(shape)
    return pl.BlockSpec(shape, lambda *_: (0,) * nd)


def _rmsnorm_rows(x, g):
    return x * lax.rsqrt(jnp.mean(x * x, axis=-1, keepdims=True) + EPS) * g


def _attn_proj_kernel(x_ref, g_ref, w_ref, qt_ref, k_ref, vt_ref, *, n_chunk, width, q_scale):
    xn = _rmsnorm_rows(x_ref[...], g_ref[...]).astype(BF16)
    tm = x_ref.shape[0]
    per_part = width // n_chunk
    heads_per_chunk = n_chunk // DA_V_DIM
    ones = jnp.ones((DA_VT_ROWS - DA_V_DIM, tm), BF16)
    for c in range(3 * per_part):
        y = jnp.dot(xn, w_ref[:, c * n_chunk:(c + 1) * n_chunk], preferred_element_type=F32)
        part, pc = divmod(c, per_part)
        if part == 0:
            qt_ref[0, pc * n_chunk:(pc + 1) * n_chunk, :] = jnp.transpose(y * q_scale).astype(BF16)
        elif part == 1:
            k_ref[:, pc * n_chunk:(pc + 1) * n_chunk] = y.astype(BF16)
        else:
            y_t = jnp.transpose(y).astype(BF16)
            for hh in range(heads_per_chunk):
                r0 = (pc * heads_per_chunk + hh) * DA_VT_ROWS
                vt_ref[0, r0:r0 + DA_V_DIM, :] = y_t[hh * DA_V_DIM:(hh + 1) * DA_V_DIM]
                vt_ref[0, r0 + DA_V_DIM:r0 + DA_VT_ROWS, :] = ones


def _attn_proj(x2, g, w_bf16, batch, seq_len):
    t, d = x2.shape
    width = w_bf16.shape[1] // 3
    n_chunk = 512
    tm = ROW_TILE
    assert seq_len % tm == 0 and width % n_chunk == 0 and width == DA_HEADS * DA_V_DIM
    tiles = seq_len // tm
    kern = functools.partial(_attn_proj_kernel, n_chunk=n_chunk, width=width,
                             q_scale=DA_HEAD_DIM ** -0.5 * LOG2E)
    return pl.pallas_call(
        kern,
        grid=(t // tm,),
        in_specs=[pl.BlockSpec((tm, d), lambda i: (i, 0)),
                  _const_spec((1, d)),
                  _const_spec((d, 3 * width))],
        out_specs=(pl.BlockSpec((1, width, tm), lambda i: (i // tiles, 0, i % tiles)),
                   pl.BlockSpec((tm, width), lambda i: (i, 0)),
                   pl.BlockSpec((1, DA_HEADS * DA_VT_ROWS, tm), lambda i: (i // tiles, 0, i % tiles))),
        out_shape=(jax.ShapeDtypeStruct((batch, width, seq_len), BF16),
                   jax.ShapeDtypeStruct((t, width), BF16),
                   jax.ShapeDtypeStruct((batch, DA_HEADS * DA_VT_ROWS, seq_len), BF16)),
        compiler_params=_params("parallel"),
        name="attn_proj",
    )(x2, g.reshape(1, d), w_bf16)


def _bucket_band(tile):
    r = np.arange(3 * tile)[:, None]
    i = np.arange(tile)[None, :]
    dist = i + 2 * tile - r
    max_exact = REL_BUCKETS // 2
    d = np.maximum(dist, 1).astype(np.float32)
    large = max_exact + (np.log(d / np.float32(max_exact)) / np.float32(math.log(REL_MAX_DIST / max_exact))
                         * np.float32(REL_BUCKETS - max_exact)).astype(np.int32)
    large = np.minimum(large, REL_BUCKETS - 1)
    bucket = np.where(dist < max_exact, dist, large)
    bucket = np.where(dist < 0, -1, bucket).astype(np.int32)
    assert (bucket[:tile] == REL_BUCKETS - 1).all()
    return bucket


def _bias_band_kernel(rel_ref, bucket_ref, o_ref):
    h = pl.program_id(0)
    bucket = bucket_ref[...]
    acc = jnp.full(bucket.shape, NEG, F32)
    for b in range(REL_BUCKETS):
        acc = jnp.where(bucket == b, rel_ref[b, h] * LOG2E, acc)
    o_ref[0] = acc


def _bias_band(rel_bias, tile):
    bucket = jnp.asarray(_bucket_band(tile))
    return pl.pallas_call(
        _bias_band_kernel,
        grid=(DA_HEADS,),
        in_specs=[pl.BlockSpec(memory_space=pltpu.SMEM),
                  _const_spec((3 * tile, tile))],
        out_specs=pl.BlockSpec((1, 3 * tile, tile), lambda h: (h, 0, 0)),
        out_shape=jax.ShapeDtypeStruct((DA_HEADS, 3 * tile, tile), F32),
        compiler_params=_params("arbitrary"),
        name="bias_band",
    )(rel_bias, bucket)


def _diff_attn_kernel(lam_ref, qt_ref, k_ref, vt_ref, band_ref, sub_ref, o_ref,
                      qz_ref, m_ref, acc_ref, sb_ref, cm_ref, *, tile, n_tiles, lambda_init):
    i = pl.program_id(1)
    head_cols = [slice(h * DA_V_DIM, (h + 1) * DA_V_DIM) for h in range(DA_HEADS)]
    depth = ATTN_PIPE_DEPTH
    slots = sb_ref.shape[0]

    sub = lax.broadcasted_iota(jnp.int32, (DA_V_DIM, tile), 0)
    first_map = (sub < DA_HEAD_DIM).astype(F32)
    for h in range(DA_HEADS):
        q_t = qt_ref[0, head_cols[h], :].astype(F32)
        qz_ref[h] = jnp.concatenate([q_t * first_map, q_t * (1.0 - first_map)], axis=1).astype(BF16)
    m_ref[...] = jnp.full(m_ref.shape, NEG, F32)
    acc_ref[...] = jnp.zeros(acc_ref.shape, F32)

    def key_rows(j):
        return pl.ds(pl.multiple_of(j * tile, tile), tile)

    def scores(j, h):
        return jnp.dot(k_ref[key_rows(j), head_cols[h]], qz_ref[h], preferred_element_type=F32)

    def drain(j, h, s_raw):
        kind = jnp.clip(j - (i - 2), 0, 2)
        bias = band_ref[h, pl.ds(pl.multiple_of(kind * tile, tile), tile), :]
        sb = s_raw + jnp.concatenate([bias, bias], axis=1)
        sb_ref[h % slots] = sb
        cm_ref[h % slots] = jnp.max(sb.reshape(tile // 8, 8, 2 * tile), axis=0)

    def consume(j, h):
        m = m_ref[h]
        m_new = jnp.maximum(m, jnp.max(cm_ref[h % slots], axis=0, keepdims=True))
        m_ref[h] = m_new
        p = jnp.exp2((sb_ref[h % slots] - m_new).astype(BF16))
        vt_blk = vt_ref[0, h * DA_VT_ROWS:(h + 1) * DA_VT_ROWS, key_rows(j)]
        return jnp.dot(vt_blk, p, preferred_element_type=F32), jnp.exp2(m - m_new)

    def rescale_add(h, pv, alpha):
        acc_ref[h] = alpha * acc_ref[h] + pv

    def item(j, n):
        return jnp.minimum(j + n // DA_HEADS, n_tiles - 1), n % DA_HEADS

    for n in range(depth):
        drain(0, n, scores(0, n))

    def tile_body(j, carry):
        issued = None
        pending = None
        for h in range(DA_HEADS):
            jn, hn = item(j, h + depth)
            nxt = (jn, hn, scores(jn, hn))
            pv, alpha = consume(j, h)
            if issued is not None:
                drain(*issued)
            if pending is not None:
                rescale_add(*pending)
            issued, pending = nxt, (h, pv, alpha)
        drain(*issued)
        rescale_add(*pending)
        return carry

    lax.fori_loop(0, i + 1, tile_body, 0)

    lp = lam_ref[...]
    lam = (jnp.exp(jnp.sum(lp[0:1] * lp[1:2], axis=-1, keepdims=True))
           - jnp.exp(jnp.sum(lp[2:3] * lp[3:4], axis=-1, keepdims=True)) + lambda_init)
    sub_w = sub_ref[...] * (1.0 - lambda_init)
    for h in range(DA_HEADS):
        o = acc_ref[h, 0:DA_V_DIM, :] * (1.0 / acc_ref[h, DA_V_DIM:DA_V_DIM + 1, :])
        o = o[:, :tile] - lam * o[:, tile:]
        o = o * lax.rsqrt(jnp.mean(o * o, axis=0, keepdims=True) + EPS)
        o_ref[0, :, head_cols[h]] = (jnp.transpose(o) * sub_w).astype(o_ref.dtype)


def _diff_attn(q_t, k, v_t, band, lam_params, subln, lambda_init):
    b, width, s = q_t.shape
    tile = ATTN_TILE
    assert s % tile == 0 and width == DA_HEADS * DA_V_DIM
    kern = functools.partial(_diff_attn_kernel, tile=tile, n_tiles=s // tile, lambda_init=lambda_init)
    return pl.pallas_call(
        kern,
        grid=(b, s // tile),
        in_specs=[_const_spec(lam_params.shape),
                  pl.BlockSpec((1, width, tile), lambda bi, i: (bi, 0, i)),
                  pl.BlockSpec((s, width), lambda bi, i: (bi, 0)),
                  pl.BlockSpec((1, DA_HEADS * DA_VT_ROWS, s), lambda bi, i: (bi, 0, 0)),
                  _const_spec(band.shape),
                  _const_spec((1, DA_V_DIM))],
        out_specs=pl.BlockSpec((1, tile, width), lambda bi, i: (bi, i, 0)),
        out_shape=jax.ShapeDtypeStruct((b, s, width), BF16),
        scratch_shapes=[pltpu.VMEM((DA_HEADS, DA_V_DIM, 2 * tile), BF16),
                        pltpu.VMEM((DA_HEADS, 1, 2 * tile), F32),
                        pltpu.VMEM((DA_HEADS, DA_VT_ROWS, 2 * tile), F32),
                        pltpu.VMEM((ATTN_RING, tile, 2 * tile), F32),
                        pltpu.VMEM((ATTN_RING, 8, 2 * tile), F32)],
        compiler_params=_params("parallel", "arbitrary"),
        name="diff_attn",
    )(lam_params, q_t, k, v_t, band, subln.reshape(1, DA_V_DIM))


def _out_mlp_kernel(a_ref, h_ref, wo_ref, g_ref, w1_ref, w2_ref, gf_ref, o_ref, u_ref,
                    *, ff_chunk, final_norm):
    h1 = h_ref[...] + jnp.dot(a_ref[...], wo_ref[...], preferred_element_type=F32)
    xn = _rmsnorm_rows(h1, g_ref[...]).astype(BF16)
    d_ff = w1_ref.shape[1]
    for c in range(d_ff // ff_chunk):
        sl = slice(c * ff_chunk, (c + 1) * ff_chunk)
        u = jnp.maximum(jnp.dot(xn, w1_ref[:, sl], preferred_element_type=F32), 0.0)
        u_ref[:, sl] = (u * u).astype(BF16)
    h2 = h1 + jnp.dot(u_ref[...], w2_ref[...], preferred_element_type=F32)
    if final_norm:
        h2 = _rmsnorm_rows(h2, gf_ref[...])
    o_ref[...] = h2


def _out_mlp(a2, h2, wo, g, w1, w2, gf, final_norm):
    t, d = h2.shape
    d_ff = w1.shape[1]
    kern = functools.partial(_out_mlp_kernel, ff_chunk=1024, final_norm=final_norm)
    single = pl.Buffered(1)
    return pl.pallas_call(
        kern,
        grid=(t // ROW_TILE,),
        in_specs=[pl.BlockSpec((ROW_TILE, a2.shape[1]), lambda i: (i, 0)),
                  pl.BlockSpec((ROW_TILE, d), lambda i: (i, 0)),
                  pl.BlockSpec(wo.shape, lambda i: (0, 0), pipeline_mode=single),
                  _const_spec((1, d)),
                  pl.BlockSpec(w1.shape, lambda i: (0, 0), pipeline_mode=single),
                  pl.BlockSpec(w2.shape, lambda i: (0, 0), pipeline_mode=single),
                  _const_spec((1, d))],
        out_specs=pl.BlockSpec((ROW_TILE, d), lambda i: (i, 0)),
        out_shape=jax.ShapeDtypeStruct((t, d), F32),
        scratch_shapes=[pltpu.VMEM((ROW_TILE, d_ff), BF16)],
        compiler_params=_params("parallel"),
        name="out_mlp_final" if final_norm else "out_mlp",
    )(a2, h2, wo, g.reshape(1, d), w1, w2, gf.reshape(1, d))


def _log_sigmoid(x):
    return -(jnp.maximum(-x, 0.0) + jnp.log(1.0 + jnp.exp(-jnp.abs(x))))


def _ml_proj_kernel(x_ref, xh_ref, g_ref, wqk_ref, wv_ref, wo_ref, wg_ref, cw_ref, cb_ref, gb_ref,
                    q_ref, k_ref, v_ref, og_ref, gcol_ref, grow_ref, cs_ref,
                    *, tiles_per_seq, q_scale):
    i = pl.program_id(0)
    tm = x_ref.shape[0]
    g = g_ref[...]
    xn = _rmsnorm_rows(x_ref[...], g).astype(BF16)
    xh = _rmsnorm_rows(xh_ref[...], g).astype(BF16)
    halo = jnp.dot(xh, wqk_ref[...], preferred_element_type=F32)
    seq_start = (i % tiles_per_seq) == 0
    cs_ref[0:HALO, :] = jnp.where(seq_start, 0.0, halo)
    cs_ref[HALO:HALO + tm, :] = jnp.dot(xn, wqk_ref[...], preferred_element_type=F32)
    conv = cb_ref[...]
    for j in range(CONV_WIDTH):
        off = HALO - (CONV_WIDTH - 1) + j
        conv = conv + cw_ref[j:j + 1, :] * cs_ref[off:off + tm, :]
    qk = conv * jax.nn.sigmoid(conv)
    half = qk.shape[1] // 2
    q_ref[...] = (qk[:, :half] * q_scale).astype(BF16)
    k_ref[...] = qk[:, half:].astype(BF16)
    v_ref[...] = jnp.dot(xn, wv_ref[...], preferred_element_type=F32).astype(BF16)
    og_ref[...] = jax.nn.sigmoid(jnp.dot(xn, wo_ref[...], preferred_element_type=F32)).astype(BF16)
    gates = jnp.dot(xn, wg_ref[...], preferred_element_type=F32) + gb_ref[...]
    lane = lax.broadcasted_iota(jnp.int32, gates.shape, 1)
    gates = jnp.where(lane < ML_HEADS, gates, _log_sigmoid(gates))
    gcol_ref[...] = gates
    grow_ref[...] = jnp.transpose(gates)[0:grow_ref.shape[0], :]


def _ml_proj(x2, xnorm_g, wqk, wv, wo, wg, conv_w, conv_b, gate_b, seq_len):
    t, d = x2.shape
    tm = ROW_TILE
    assert seq_len % tm == 0 and tm % HALO == 0
    qk_w = wqk.shape[1]
    v_w = wv.shape[1]
    kern = functools.partial(_ml_proj_kernel, tiles_per_seq=seq_len // tm, q_scale=ML_QK_DIM ** -0.5)
    single = pl.Buffered(1)
    halo_blocks = tm // HALO
    out_shapes = (jax.ShapeDtypeStruct((t, qk_w // 2), BF16),
                  jax.ShapeDtypeStruct((t, qk_w // 2), BF16),
                  jax.ShapeDtypeStruct((t, v_w), BF16),
                  jax.ShapeDtypeStruct((t, v_w), BF16),
                  jax.ShapeDtypeStruct((t, ML_GATE_PAD), F32),
                  jax.ShapeDtypeStruct((2 * ML_HEADS, t), F32))
    return pl.pallas_call(
        kern,
        grid=(t // tm,),
        in_specs=[pl.BlockSpec((tm, d), lambda i: (i, 0)),
                  pl.BlockSpec((HALO, d), lambda i: (jnp.maximum(i * halo_blocks - 1, 0), 0)),
                  _const_spec((1, d)),
                  pl.BlockSpec(wqk.shape, lambda i: (0, 0), pipeline_mode=single),
                  pl.BlockSpec(wv.shape, lambda i: (0, 0), pipeline_mode=single),
                  pl.BlockSpec(wo.shape, lambda i: (0, 0), pipeline_mode=single),
                  _const_spec(wg.shape),
                  _const_spec(conv_w.shape),
                  _const_spec((1, qk_w)),
                  _const_spec((1, ML_GATE_PAD))],
        out_specs=(pl.BlockSpec((tm, qk_w // 2), lambda i: (i, 0)),
                   pl.BlockSpec((tm, qk_w // 2), lambda i: (i, 0)),
                   pl.BlockSpec((tm, v_w), lambda i: (i, 0)),
                   pl.BlockSpec((tm, v_w), lambda i: (i, 0)),
                   pl.BlockSpec((tm, ML_GATE_PAD), lambda i: (i, 0)),
                   pl.BlockSpec((2 * ML_HEADS, tm), lambda i: (0, i))),
        out_shape=out_shapes,
        scratch_shapes=[pltpu.VMEM((HALO + tm, qk_w), F32)],
        compiler_params=_params("parallel"),
        name="mlstm_proj",
    )(x2, x2, xnorm_g.reshape(1, d), wqk, wv, wo, wg, conv_w, conv_b.reshape(1, qk_w),
      gate_b.reshape(1, ML_GATE_PAD))


def _mlstm_kernel(q_ref, k_ref, v_ref, og_ref, gcol_ref, grow_ref, hn_ref, o_ref,
                  c_ref, n_ref, m_ref):
    @pl.when(pl.program_id(1) == 0)
    def _():
        c_ref[...] = jnp.zeros_like(c_ref)
        n_ref[...] = jnp.zeros_like(n_ref)
        m_ref[...] = jnp.zeros_like(m_ref)

    L = q_ref.shape[0]
    row = lax.broadcasted_iota(jnp.int32, (L, L), 0)
    col = lax.broadcasted_iota(jnp.int32, (L, L), 1)
    causal = col <= row
    gcol = gcol_ref[...]
    grow = grow_ref[...]
    for h in range(ML_HEADS):
        qh = q_ref[:, h * ML_QK_DIM:(h + 1) * ML_QK_DIM]
        kh = k_ref[:, h * ML_QK_DIM:(h + 1) * ML_QK_DIM]
        vcols = slice(h * ML_V_DIM, (h + 1) * ML_V_DIM)
        vh = v_ref[:, vcols]
        ig_r = grow[h:h + 1, :]
        lf_r = grow[ML_HEADS + h:ML_HEADS + h + 1, :]
        ig_c = gcol[:, h:h + 1]
        lf_c = gcol[:, ML_HEADS + h:ML_HEADS + h + 1]
        m_prev = m_ref[h]
        n_prev = n_ref[h]
        c_prev = c_ref[h]

        b_c = jnp.sum(jnp.where(causal, lf_r, 0.0), axis=1, keepdims=True)
        b_r = jnp.sum(jnp.where(causal, 0.0, lf_c), axis=0, keepdims=True) + lf_r
        b_last = jnp.sum(lf_r, axis=1, keepdims=True)

        d_mat = jnp.where(causal, b_c - b_r + ig_r, NEG)
        m_inter = b_c + m_prev
        m_j = jnp.maximum(m_inter, jnp.max(d_mat, axis=1, keepdims=True))
        s_qk = lax.dot_general(qh, kh, (((1,), (1,)), ((), ())), preferred_element_type=F32)
        sc = s_qk * jnp.exp(d_mat - m_j)
        inter = jnp.exp(m_inter - m_j)
        num = (jnp.dot(sc.astype(BF16), vh, preferred_element_type=F32)
               + inter * jnp.dot(qh, c_prev.astype(BF16), preferred_element_type=F32))
        den = (jnp.sum(sc, axis=1, keepdims=True)
               + inter * jnp.sum(qh.astype(F32) * n_prev, axis=1, keepdims=True))
        hcore = num / jnp.maximum(jnp.abs(den), jnp.exp(-m_j))
        y = _rmsnorm_rows(hcore, hn_ref[:, vcols])
        o_ref[:, vcols] = (og_ref[:, vcols].astype(F32) * y).astype(o_ref.dtype)

        a_c = b_last - b_c + ig_c
        m_loc = jnp.max(a_c, axis=0, keepdims=True)
        m_new = jnp.maximum(b_last + m_prev, m_loc)
        w_c = jnp.exp(a_c - m_new)
        decay = jnp.exp(b_last + m_prev - m_new)
        wv = (w_c * vh.astype(F32)).astype(BF16)
        c_ref[h] = decay * c_prev + lax.dot_general(
            kh, wv, (((0,), (0,)), ((), ())), preferred_element_type=F32)
        n_ref[h] = decay * n_prev + jnp.sum(w_c * kh.astype(F32), axis=0, keepdims=True)
        m_ref[h] = m_new


def _mlstm(q, k, v, og, gcol, grow, head_norm, batch, seq_len):
    t = q.shape[0]
    L = ML_CHUNK
    assert seq_len % L == 0
    nc = seq_len // L
    v_w = v.shape[1]
    row_map = lambda b, c: (b * nc + c, 0)
    return pl.pallas_call(
        _mlstm_kernel,
        grid=(batch, nc),
        in_specs=[pl.BlockSpec((L, q.shape[1]), row_map),
                  pl.BlockSpec((L, k.shape[1]), row_map),
                  pl.BlockSpec((L, v_w), row_map),
                  pl.BlockSpec((L, v_w), row_map),
                  pl.BlockSpec((L, ML_GATE_PAD), row_map),
                  pl.BlockSpec((2 * ML_HEADS, L), lambda b, c: (0, b * nc + c)),
                  _const_spec((1, v_w))],
        out_specs=pl.BlockSpec((L, v_w), row_map),
        out_shape=jax.ShapeDtypeStruct((t, v_w), BF16),
        scratch_shapes=[pltpu.VMEM((ML_HEADS, ML_QK_DIM, ML_V_DIM), F32),
                        pltpu.VMEM((ML_HEADS, 1, ML_QK_DIM), F32),
                        pltpu.VMEM((ML_HEADS, 1, 1), F32)],
        compiler_params=_params("parallel", "arbitrary"),
        name="mlstm_chunk",
    )(q, k, v, og, gcol, grow, head_norm.reshape(1, v_w))


def kernel(x, rel_bias, attn_norm, attn_w_in, attn_lambda_q1, attn_lambda_k1, attn_lambda_q2,
           attn_lambda_k2, attn_subln, attn_w_out, mlstm_norm, mlstm_w_in, mlstm_b_i, mlstm_b_f,
           mlstm_conv_w, mlstm_conv_b, mlstm_head_norm, mlstm_w_out, mlp_norm, mlp_w1, mlp_w2,
           final_norm):
    batch, seq_len, d = x.shape
    t = batch * seq_len
    h = x.reshape(t, d)

    lambda_init = 0.8 - 0.6 * math.exp(-0.3 * 0)
    q_t, k, v_t = _attn_proj(h, attn_norm[0], attn_w_in[0].astype(BF16), batch, seq_len)
    band = _bias_band(rel_bias, ATTN_TILE)
    lam_params = jnp.stack([attn_lambda_q1[0], attn_lambda_k1[0],
                            attn_lambda_q2[0], attn_lambda_k2[0]]).astype(F32)
    attn = _diff_attn(q_t, k, v_t, band, lam_params, attn_subln[0], lambda_init)
    h = _out_mlp(attn.reshape(t, -1), h, attn_w_out[0].astype(BF16), mlp_norm[0],
                 mlp_w1[0].astype(BF16), mlp_w2[0].astype(BF16), final_norm, final_norm=False)

    qk_w = 2 * ML_HEADS * ML_QK_DIM
    v_w = ML_HEADS * ML_V_DIM
    w_in = mlstm_w_in[0]
    wqk = w_in[:, :qk_w].astype(BF16)
    wv = w_in[:, qk_w:qk_w + v_w].astype(BF16)
    wo = w_in[:, qk_w + v_w:qk_w + 2 * v_w].astype(BF16)
    n_gates = 2 * ML_HEADS
    wg = jnp.pad(w_in[:, qk_w + 2 * v_w:], ((0, 0), (0, ML_GATE_PAD - n_gates))).astype(BF16)
    gate_b = jnp.pad(jnp.concatenate([mlstm_b_i[0], mlstm_b_f[0]]).astype(F32),
                     (0, ML_GATE_PAD - n_gates))
    q, k, v, og, gcol, grow = _ml_proj(h, mlstm_norm[0], wqk, wv, wo, wg, mlstm_conv_w[0],
                                       mlstm_conv_b[0], gate_b, seq_len)
    mix = _mlstm(q, k, v, og, gcol, grow, mlstm_head_norm[0], batch, seq_len)
    out = _out_mlp(mix, h, mlstm_w_out[0].astype(BF16), mlp_norm[1],
                   mlp_w1[1].astype(BF16), mlp_w2[1].astype(BF16), final_norm, final_norm=True)
    return out.reshape(batch, seq_len, d)
```

```python
import functools
import math

import numpy as np
import jax
import jax.numpy as jnp
from jax import lax
from jax.experimental import pallas as pl
from jax.experimental.pallas import tpu as pltpu

F32 = jnp.float32
BF16 = jnp.bfloat16

EPS = 1e-6
NEG = -1e30

DA_HEADS = 8
DA_HEAD_DIM = 64
DA_V_DIM = 2 * DA_HEAD_DIM
REL_BUCKETS = 32
REL_MAX_DIST = 128
DA_VT_ROWS = DA_V_DIM + 16
LOG2E = math.log2(math.e)

ML_HEADS = 4
ML_QK_DIM = 128
ML_V_DIM = 256
CONV_WIDTH = 4
ML_GATE_PAD = 128
ML_VT_ROWS = ML_V_DIM + 16

LANES = 128
BF16_SUBLANES = 16
VMEM_LIMIT = 56 * 1024 * 1024

ROW_TILE = 512
ATTN_TILE = 256
ML_CHUNK = 256
ML_PROJ_CHUNK = 256
ML_PROJ_AHEAD = 2
ATTN_PIPE_DEPTH = 4
ATTN_RING = 4
HALO = BF16_SUBLANES


def _params(*sem):
    return pltpu.CompilerParams(dimension_semantics=sem, vmem_limit_bytes=VMEM_LIMIT)


def _const_spec(shape):
    nd = len(shape)
    return pl.BlockSpec(shape, lambda *_: (0,) * nd)


def _rmsnorm_rows(x, g):
    return x * lax.rsqrt(jnp.mean(x * x, axis=-1, keepdims=True) + EPS) * g


def _attn_proj_kernel(x_ref, g_ref, w_ref, qt_ref, k_ref, vt_ref, *, n_chunk, width, q_scale):
    xn = _rmsnorm_rows(x_ref[...], g_ref[...]).astype(BF16)
    tm = x_ref.shape[0]
    per_part = width // n_chunk
    heads_per_chunk = n_chunk // DA_V_DIM
    ones = jnp.ones((DA_VT_ROWS - DA_V_DIM, tm), BF16)
    for c in range(3 * per_part):
        y = jnp.dot(xn, w_ref[:, c * n_chunk:(c + 1) * n_chunk], preferred_element_type=F32)
        part, pc = divmod(c, per_part)
        if part == 0:
            qt_ref[0, pc * n_chunk:(pc + 1) * n_chunk, :] = jnp.transpose(y * q_scale).astype(BF16)
        elif part == 1:
            k_ref[:, pc * n_chunk:(pc + 1) * n_chunk] = y.astype(BF16)
        else:
            y_t = jnp.transpose(y).astype(BF16)
            for hh in range(heads_per_chunk):
                r0 = (pc * heads_per_chunk + hh) * DA_VT_ROWS
                vt_ref[0, r0:r0 + DA_V_DIM, :] = y_t[hh * DA_V_DIM:(hh + 1) * DA_V_DIM]
                vt_ref[0, r0 + DA_V_DIM:r0 + DA_VT_ROWS, :] = ones


def _attn_proj(x2, g, w_bf16, batch, seq_len):
    t, d = x2.shape
    width = w_bf16.shape[1] // 3
    n_chunk = 512
    tm = ROW_TILE
    assert seq_len % tm == 0 and width % n_chunk == 0 and width == DA_HEADS * DA_V_DIM
    tiles = seq_len // tm
    kern = functools.partial(_attn_proj_kernel, n_chunk=n_chunk, width=width,
                             q_scale=DA_HEAD_DIM ** -0.5 * LOG2E)
    return pl.pallas_call(
        kern,
        grid=(t // tm,),
        in_specs=[pl.BlockSpec((tm, d), lambda i: (i, 0)),
                  _const_spec((1, d)),
                  _const_spec((d, 3 * width))],
        out_specs=(pl.BlockSpec((1, width, tm), lambda i: (i // tiles, 0, i % tiles)),
                   pl.BlockSpec((tm, width), lambda i: (i, 0)),
                   pl.BlockSpec((1, DA_HEADS * DA_VT_ROWS, tm), lambda i: (i // tiles, 0, i % tiles))),
        out_shape=(jax.ShapeDtypeStruct((batch, width, seq_len), BF16),
                   jax.ShapeDtypeStruct((t, width), BF16),
                   jax.ShapeDtypeStruct((batch, DA_HEADS * DA_VT_ROWS, seq_len), BF16)),
        compiler_params=_params("parallel"),
        name="attn_proj",
    )(x2, g.reshape(1, d), w_bf16)


def _bucket_band(tile):
    r = np.arange(3 * tile)[:, None]
    i = np.arange(tile)[None, :]
    dist = i + 2 * tile - r
    max_exact = REL_BUCKETS // 2
    d = np.maximum(dist, 1).astype(np.float32)
    large = max_exact + (np.log(d / np.float32(max_exact)) / np.float32(math.log(REL_MAX_DIST / max_exact))
                         * np.float32(REL_BUCKETS - max_exact)).astype(np.int32)
    large = np.minimum(large, REL_BUCKETS - 1)
    bucket = np.where(dist < max_exact, dist, large)
    bucket = np.where(dist < 0, -1, bucket).astype(np.int32)
    assert (bucket[:tile] == REL_BUCKETS - 1).all()
    return bucket


def _bias_band_kernel(rel_ref, bucket_ref, o_ref):
    h = pl.program_id(0)
    bucket = bucket_ref[...]
    acc = jnp.full(bucket.shape, NEG, F32)
    for b in range(REL_BUCKETS):
        acc = jnp.where(bucket == b, rel_ref[b, h] * LOG2E, acc)
    o_ref[0] = acc


def _bias_band(rel_bias, tile):
    bucket = jnp.asarray(_bucket_band(tile))
    return pl.pallas_call(
        _bias_band_kernel,
        grid=(DA_HEADS,),
        in_specs=[pl.BlockSpec(memory_space=pltpu.SMEM),
                  _const_spec((3 * tile, tile))],
        out_specs=pl.BlockSpec((1, 3 * tile, tile), lambda h: (h, 0, 0)),
        out_shape=jax.ShapeDtypeStruct((DA_HEADS, 3 * tile, tile), F32),
        compiler_params=_params("arbitrary"),
        name="bias_band",
    )(rel_bias, bucket)


def _diff_attn_kernel(lam_ref, qt_ref, k_ref, vt_ref, band_ref, sub_ref, o_ref,
                      qz_ref, m_ref, acc_ref, sb_ref, cm_ref, *, tile, n_tiles, lambda_init):
    i = pl.program_id(1)
    head_cols = [slice(h * DA_V_DIM, (h + 1) * DA_V_DIM) for h in range(DA_HEADS)]
    depth = ATTN_PIPE_DEPTH
    slots = sb_ref.shape[0]

    sub = lax.broadcasted_iota(jnp.int32, (DA_V_DIM, tile), 0)
    first_map = (sub < DA_HEAD_DIM).astype(F32)
    for h in range(DA_HEADS):
        q_t = qt_ref[0, head_cols[h], :].astype(F32)
        qz_ref[h] = jnp.concatenate([q_t * first_map, q_t * (1.0 - first_map)], axis=1).astype(BF16)
    m_ref[...] = jnp.full(m_ref.shape, NEG, F32)
    acc_ref[...] = jnp.zeros(acc_ref.shape, F32)

    def key_rows(j):
        return pl.ds(pl.multiple_of(j * tile, tile), tile)

    def scores(j, h):
        return jnp.dot(k_ref[key_rows(j), head_cols[h]], qz_ref[h], preferred_element_type=F32)

    def drain(j, h, s_raw):
        kind = jnp.clip(j - (i - 2), 0, 2)
        bias = band_ref[h, pl.ds(pl.multiple_of(kind * tile, tile), tile), :]
        sb = s_raw + jnp.concatenate([bias, bias], axis=1)
        sb_ref[h % slots] = sb
        cm_ref[h % slots] = jnp.max(sb.reshape(tile // 8, 8, 2 * tile), axis=0)

    def consume(j, h):
        m = m_ref[h]
        m_new = jnp.maximum(m, jnp.max(cm_ref[h % slots], axis=0, keepdims=True))
        m_ref[h] = m_new
        p = jnp.exp2((sb_ref[h % slots] - m_new).astype(BF16))
        vt_blk = vt_ref[0, h * DA_VT_ROWS:(h + 1) * DA_VT_ROWS, key_rows(j)]
        return jnp.dot(vt_blk, p, preferred_element_type=F32), jnp.exp2(m - m_new)

    def rescale_add(h, pv, alpha):
        acc_ref[h] = alpha * acc_ref[h] + pv

    def item(j, n):
        return jnp.minimum(j + n // DA_HEADS, n_tiles - 1), n % DA_HEADS

    for n in range(depth):
        drain(0, n, scores(0, n))

    def tile_body(j, carry):
        issued = None
        pending = None
        for h in range(DA_HEADS):
            jn, hn = item(j, h + depth)
            nxt = (jn, hn, scores(jn, hn))
            pv, alpha = consume(j, h)
            if issued is not None:
                drain(*issued)
            if pending is not None:
                rescale_add(*pending)
            issued, pending = nxt, (h, pv, alpha)
        drain(*issued)
        rescale_add(*pending)
        return carry

    lax.fori_loop(0, i + 1, tile_body, 0)

    lp = lam_ref[...]
    lam = (jnp.exp(jnp.sum(lp[0:1] * lp[1:2], axis=-1, keepdims=True))
           - jnp.exp(jnp.sum(lp[2:3] * lp[3:4], axis=-1, keepdims=True)) + lambda_init)
    sub_w = sub_ref[...] * (1.0 - lambda_init)
    for h in range(DA_HEADS):
        o = acc_ref[h, 0:DA_V_DIM, :] * (1.0 / acc_ref[h, DA_V_DIM:DA_V_DIM + 1, :])
        o = o[:, :tile] - lam * o[:, tile:]
        o = o * lax.rsqrt(jnp.mean(o * o, axis=0, keepdims=True) + EPS)
        o_ref[0, :, head_cols[h]] = (jnp.transpose(o) * sub_w).astype(o_ref.dtype)


def _diff_attn(q_t, k, v_t, band, lam_params, subln, lambda_init):
    b, width, s = q_t.shape
    tile = ATTN_TILE
    assert s % tile == 0 and width == DA_HEADS * DA_V_DIM
    kern = functools.partial(_diff_attn_kernel, tile=tile, n_tiles=s // tile, lambda_init=lambda_init)
    return pl.pallas_call(
        kern,
        grid=(b, s // tile),
        in_specs=[_const_spec(lam_params.shape),
                  pl.BlockSpec((1, width, tile), lambda bi, i: (bi, 0, i)),
                  pl.BlockSpec((s, width), lambda bi, i: (bi, 0)),
                  pl.BlockSpec((1, DA_HEADS * DA_VT_ROWS, s), lambda bi, i: (bi, 0, 0)),
                  _const_spec(band.shape),
                  _const_spec((1, DA_V_DIM))],
        out_specs=pl.BlockSpec((1, tile, width), lambda bi, i: (bi, i, 0)),
        out_shape=jax.ShapeDtypeStruct((b, s, width), BF16),
        scratch_shapes=[pltpu.VMEM((DA_HEADS, DA_V_DIM, 2 * tile), BF16),
                        pltpu.VMEM((DA_HEADS, 1, 2 * tile), F32),
                        pltpu.VMEM((DA_HEADS, DA_VT_ROWS, 2 * tile), F32),
                        pltpu.VMEM((ATTN_RING, tile, 2 * tile), F32),
                        pltpu.VMEM((ATTN_RING, 8, 2 * tile), F32)],
        compiler_params=_params("parallel", "arbitrary"),
        name="diff_attn",
    )(lam_params, q_t, k, v_t, band, subln.reshape(1, DA_V_DIM))


def _out_mlp_kernel(a_ref, h_ref, wo_ref, g_ref, w1_ref, w2_ref, gf_ref, o_ref, u_ref,
                    *, ff_chunk, final_norm):
    h1 = h_ref[...] + jnp.dot(a_ref[...], wo_ref[...], preferred_element_type=F32)
    xn = _rmsnorm_rows(h1, g_ref[...]).astype(BF16)
    d_ff = w1_ref.shape[1]
    for c in range(d_ff // ff_chunk):
        sl = slice(c * ff_chunk, (c + 1) * ff_chunk)
        u = jnp.maximum(jnp.dot(xn, w1_ref[:, sl], preferred_element_type=F32), 0.0)
        u_ref[:, sl] = (u * u).astype(BF16)
    h2 = h1 + jnp.dot(u_ref[...], w2_ref[...], preferred_element_type=F32)
    if final_norm:
        h2 = _rmsnorm_rows(h2, gf_ref[...])
    o_ref[...] = h2


def _out_mlp(a2, h2, wo, g, w1, w2, gf, final_norm):
    t, d = h2.shape
    d_ff = w1.shape[1]
    kern = functools.partial(_out_mlp_kernel, ff_chunk=1024, final_norm=final_norm)
    single = pl.Buffered(1)
    return pl.pallas_call(
        kern,
        grid=(t // ROW_TILE,),
        in_specs=[pl.BlockSpec((ROW_TILE, a2.shape[1]), lambda i: (i, 0)),
                  pl.BlockSpec((ROW_TILE, d), lambda i: (i, 0)),
                  pl.BlockSpec(wo.shape, lambda i: (0, 0), pipeline_mode=single),
                  _const_spec((1, d)),
                  pl.BlockSpec(w1.shape, lambda i: (0, 0), pipeline_mode=single),
                  pl.BlockSpec(w2.shape, lambda i: (0, 0), pipeline_mode=single),
                  _const_spec((1, d))],
        out_specs=pl.BlockSpec((ROW_TILE, d), lambda i: (i, 0)),
        out_shape=jax.ShapeDtypeStruct((t, d), F32),
        scratch_shapes=[pltpu.VMEM((ROW_TILE, d_ff), BF16)],
        compiler_params=_params("parallel"),
        name="out_mlp_final" if final_norm else "out_mlp",
    )(a2, h2, wo, g.reshape(1, d), w1, w2, gf.reshape(1, d))


def _log_sigmoid(x):
    return -(jnp.maximum(-x, 0.0) + jnp.log(1.0 + jnp.exp(-jnp.abs(x))))


def _ml_proj_kernel(x_ref, xh_ref, g_ref, w_ref, cw_ref, cb_ref, gb_ref,
                    qt_ref, k_ref, vt_ref, og_ref, gcol_ref, grow_ref, cs_ref,
                    *, tiles_per_seq, q_scale, qk_w, v_w):
    i = pl.program_id(0)
    tm = x_ref.shape[0]
    nc = ML_PROJ_CHUNK
    g = g_ref[...]
    xn = _rmsnorm_rows(x_ref[...], g).astype(BF16)
    xh = _rmsnorm_rows(xh_ref[...], g).astype(BF16)
    seq_start = (i % tiles_per_seq) == 0
    ones = jnp.ones((ML_VT_ROWS - ML_V_DIM, tm), BF16)
    n_chunks = (qk_w + 2 * v_w) // nc

    def matmul(c):
        if c == n_chunks:
            return jnp.dot(xn, w_ref[:, c * nc:c * nc + ML_GATE_PAD], preferred_element_type=F32)
        return jnp.dot(xn, w_ref[:, c * nc:(c + 1) * nc], preferred_element_type=F32)

    def epilogue(c, y):
        col0 = c * nc
        cols = slice(col0, col0 + nc)
        if col0 < qk_w:
            slot = c % 2
            halo = jnp.dot(xh, w_ref[:, cols], preferred_element_type=F32)
            cs_ref[slot, 0:HALO, :] = jnp.where(seq_start, 0.0, halo)
            cs_ref[slot, HALO:HALO + tm, :] = y
            conv = cb_ref[:, cols]
            for j in range(CONV_WIDTH):
                off = HALO - (CONV_WIDTH - 1) + j
                conv = conv + cw_ref[j:j + 1, cols] * cs_ref[slot, off:off + tm, :]
            act = conv * jax.nn.sigmoid(conv)
            if col0 < qk_w // 2:
                qt_ref[0, cols, :] = jnp.transpose(act * q_scale).astype(BF16)
            else:
                k_ref[:, col0 - qk_w // 2:col0 - qk_w // 2 + nc] = act.astype(BF16)
        elif col0 < qk_w + v_w:
            r0 = ((col0 - qk_w) // ML_V_DIM) * ML_VT_ROWS
            vt_ref[0, r0:r0 + ML_V_DIM, :] = jnp.transpose(y).astype(BF16)
            vt_ref[0, r0 + ML_V_DIM:r0 + ML_VT_ROWS, :] = ones
        else:
            o0 = col0 - qk_w - v_w
            og_ref[:, o0:o0 + nc] = jax.nn.sigmoid(y).astype(BF16)

    qk_chunks = list(range(qk_w // nc))
    light = list(range(qk_w // nc, n_chunks))
    order = []
    while qk_chunks or light:
        order += light[:1] + qk_chunks[:1] + light[1:2]
        qk_chunks, light = qk_chunks[1:], light[2:]
    order.append(n_chunks)
    results = [matmul(c) for c in order[:ML_PROJ_AHEAD]]
    for pos, c in enumerate(order[:-1]):
        if pos + ML_PROJ_AHEAD < len(order):
            results.append(matmul(order[pos + ML_PROJ_AHEAD]))
        epilogue(c, results[pos])
    y = results[-1]
    gates = y + gb_ref[...]
    lane = lax.broadcasted_iota(jnp.int32, gates.shape, 1)
    gates = jnp.where(lane < ML_HEADS, gates, _log_sigmoid(gates))
    gcol_ref[...] = gates
    grow_ref[...] = jnp.transpose(gates)[0:grow_ref.shape[0], :]


def _ml_proj(x2, xnorm_g, w_all, conv_w, conv_b, gate_b, batch, seq_len):
    t, d = x2.shape
    tm = ROW_TILE
    qk_w = 2 * ML_HEADS * ML_QK_DIM
    v_w = ML_HEADS * ML_V_DIM
    assert seq_len % tm == 0 and tm % HALO == 0 and ML_PROJ_CHUNK == ML_V_DIM
    assert w_all.shape[1] == qk_w + 2 * v_w + ML_GATE_PAD
    tiles = seq_len // tm
    kern = functools.partial(_ml_proj_kernel, tiles_per_seq=tiles, q_scale=ML_QK_DIM ** -0.5,
                             qk_w=qk_w, v_w=v_w)
    halo_blocks = tm // HALO
    out_shapes = (jax.ShapeDtypeStruct((batch, qk_w // 2, seq_len), BF16),
                  jax.ShapeDtypeStruct((t, qk_w // 2), BF16),
                  jax.ShapeDtypeStruct((batch, ML_HEADS * ML_VT_ROWS, seq_len), BF16),
                  jax.ShapeDtypeStruct((t, v_w), BF16),
                  jax.ShapeDtypeStruct((t, ML_GATE_PAD), F32),
                  jax.ShapeDtypeStruct((2 * ML_HEADS, t), F32))
    return pl.pallas_call(
        kern,
        grid=(t // tm,),
        in_specs=[pl.BlockSpec((tm, d), lambda i: (i, 0)),
                  pl.BlockSpec((HALO, d), lambda i: (jnp.maximum(i * halo_blocks - 1, 0), 0)),
                  _const_spec((1, d)),
                  pl.BlockSpec(w_all.shape, lambda i: (0, 0), pipeline_mode=pl.Buffered(1)),
                  _const_spec(conv_w.shape),
                  _const_spec((1, qk_w)),
                  _const_spec((1, ML_GATE_PAD))],
        out_specs=(pl.BlockSpec((1, qk_w // 2, tm), lambda i: (i // tiles, 0, i % tiles)),
                   pl.BlockSpec((tm, qk_w // 2), lambda i: (i, 0)),
                   pl.BlockSpec((1, ML_HEADS * ML_VT_ROWS, tm), lambda i: (i // tiles, 0, i % tiles)),
                   pl.BlockSpec((tm, v_w), lambda i: (i, 0)),
                   pl.BlockSpec((tm, ML_GATE_PAD), lambda i: (i, 0)),
                   pl.BlockSpec((2 * ML_HEADS, tm), lambda i: (0, i))),
        out_shape=out_shapes,
        scratch_shapes=[pltpu.VMEM((2, HALO + tm, ML_PROJ_CHUNK), F32)],
        compiler_params=_params("parallel"),
        name="mlstm_proj",
    )(x2, x2, xnorm_g.reshape(1, d), w_all, conv_w, conv_b.reshape(1, qk_w),
      gate_b.reshape(1, ML_GATE_PAD))


def _mlstm_kernel(qt_ref, k_ref, vt_ref, og_ref, gcol_ref, grow_ref, hn_ref, o_ref, s_ref, m_ref):
    @pl.when(pl.program_id(1) == 0)
    def _():
        s_ref[...] = jnp.zeros_like(s_ref)
        m_ref[...] = jnp.zeros_like(m_ref)

    L = k_ref.shape[0]
    row = lax.broadcasted_iota(jnp.int32, (L, L), 0)
    col = lax.broadcasted_iota(jnp.int32, (L, L), 1)
    key_le_query = row <= col
    gcol = gcol_ref[...]
    grow = grow_ref[...]

    def qk_blocks(h):
        return (qt_ref[0, h * ML_QK_DIM:(h + 1) * ML_QK_DIM, :],
                k_ref[:, h * ML_QK_DIM:(h + 1) * ML_QK_DIM])

    def matmuls(h):
        q_t, kh = qk_blocks(h)
        s_t = jnp.dot(kh, q_t, preferred_element_type=F32)
        sq = jnp.dot(s_ref[h].astype(BF16), q_t, preferred_element_type=F32)
        return s_t, sq

    def head(h, s_t, sq):
        _, kh = qk_blocks(h)
        vcols = slice(h * ML_V_DIM, (h + 1) * ML_V_DIM)
        v_t = vt_ref[0, h * ML_VT_ROWS:(h + 1) * ML_VT_ROWS, :]
        ig_r = grow[h:h + 1, :]
        lf_r = grow[ML_HEADS + h:ML_HEADS + h + 1, :]
        ig_c = gcol[:, h:h + 1]
        lf_c = gcol[:, ML_HEADS + h:ML_HEADS + h + 1]
        m_prev = m_ref[h]

        b_r = jnp.sum(jnp.where(key_le_query, lf_c, 0.0), axis=0, keepdims=True)
        b_c = jnp.sum(jnp.where(key_le_query, 0.0, lf_r), axis=1, keepdims=True) + lf_c
        b_last = jnp.sum(lf_r, axis=1, keepdims=True)

        d_t = jnp.where(key_le_query, b_r - (b_c - ig_c), NEG)
        m_inter = b_r + m_prev
        m_j = jnp.maximum(m_inter, jnp.max(d_t, axis=0, keepdims=True))
        sc_t = s_t * jnp.exp(d_t - m_j)
        inter = jnp.exp(m_inter - m_j)
        tot = jnp.dot(v_t, sc_t.astype(BF16), preferred_element_type=F32) + inter * sq
        den = tot[ML_V_DIM:ML_V_DIM + 1, :]
        h_t = tot[0:ML_V_DIM, :] * (1.0 / jnp.maximum(jnp.abs(den), jnp.exp(-m_j)))
        y_t = h_t * lax.rsqrt(jnp.mean(h_t * h_t, axis=0, keepdims=True) + EPS)
        y = jnp.transpose(y_t) * hn_ref[:, vcols]
        o_ref[:, vcols] = (og_ref[:, vcols].astype(F32) * y).astype(o_ref.dtype)

        a_r = b_last - b_r + ig_r
        m_new = jnp.maximum(b_last + m_prev, jnp.max(a_r, axis=1, keepdims=True))
        w_r = jnp.exp(a_r - m_new)
        decay = jnp.exp(b_last + m_prev - m_new)
        wv_t = (v_t.astype(F32) * w_r).astype(BF16)
        s_ref[h] = decay * s_ref[h] + jnp.dot(wv_t, kh, preferred_element_type=F32)
        m_ref[h] = m_new

    ahead = 2
    ready = [matmuls(h) for h in range(ahead)]
    for h in range(ML_HEADS):
        if h + ahead < ML_HEADS:
            ready.append(matmuls(h + ahead))
        head(h, *ready[h])


def _mlstm(q_t, k, v_t, og, gcol, grow, head_norm, batch, seq_len):
    L = ML_CHUNK
    assert seq_len % L == 0
    nc = seq_len // L
    v_w = og.shape[1]
    row_map = lambda b, c: (b * nc + c, 0)
    lane_map = lambda b, c: (b, 0, c)
    return pl.pallas_call(
        _mlstm_kernel,
        grid=(batch, nc),
        in_specs=[pl.BlockSpec((1, q_t.shape[1], L), lane_map),
                  pl.BlockSpec((L, k.shape[1]), row_map),
                  pl.BlockSpec((1, v_t.shape[1], L), lane_map),
                  pl.BlockSpec((L, v_w), row_map),
                  pl.BlockSpec((L, ML_GATE_PAD), row_map),
                  pl.BlockSpec((2 * ML_HEADS, L), lambda b, c: (0, b * nc + c)),
                  _const_spec((1, v_w))],
        out_specs=pl.BlockSpec((L, v_w), row_map),
        out_shape=jax.ShapeDtypeStruct((batch * seq_len, v_w), BF16),
        scratch_shapes=[pltpu.VMEM((ML_HEADS, ML_VT_ROWS, ML_QK_DIM), F32),
                        pltpu.VMEM((ML_HEADS, 1, 1), F32)],
        compiler_params=_params("parallel", "arbitrary"),
        name="mlstm_chunk",
    )(q_t, k, v_t, og, gcol, grow, head_norm.reshape(1, v_w))


def kernel(x, rel_bias, attn_norm, attn_w_in, attn_lambda_q1, attn_lambda_k1, attn_lambda_q2,
           attn_lambda_k2, attn_subln, attn_w_out, mlstm_norm, mlstm_w_in, mlstm_b_i, mlstm_b_f,
           mlstm_conv_w, mlstm_conv_b, mlstm_head_norm, mlstm_w_out, mlp_norm, mlp_w1, mlp_w2,
           final_norm):
    batch, seq_len, d = x.shape
    t = batch * seq_len
    h = x.reshape(t, d)

    lambda_init = 0.8 - 0.6 * math.exp(-0.3 * 0)
    q_t, k, v_t = _attn_proj(h, attn_norm[0], attn_w_in[0].astype(BF16), batch, seq_len)
    band = _bias_band(rel_bias, ATTN_TILE)
    lam_params = jnp.stack([attn_lambda_q1[0], attn_lambda_k1[0],
                            attn_lambda_q2[0], attn_lambda_k2[0]]).astype(F32)
    attn = _diff_attn(q_t, k, v_t, band, lam_params, attn_subln[0], lambda_init)
    h = _out_mlp(attn.reshape(t, -1), h, attn_w_out[0].astype(BF16), mlp_norm[0],
                 mlp_w1[0].astype(BF16), mlp_w2[0].astype(BF16), final_norm, final_norm=False)

    n_gates = 2 * ML_HEADS
    w_all = jnp.pad(mlstm_w_in[0], ((0, 0), (0, ML_GATE_PAD - n_gates))).astype(BF16)
    gate_b = jnp.pad(jnp.concatenate([mlstm_b_i[0], mlstm_b_f[0]]).astype(F32),
                     (0, ML_GATE_PAD - n_gates))
    q_t, k, v_t, og, gcol, grow = _ml_proj(h, mlstm_norm[0], w_all, mlstm_conv_w[0],
                                           mlstm_conv_b[0], gate_b, batch, seq_len)
    mix = _mlstm(q_t, k, v_t, og, gcol, grow, mlstm_head_norm[0], batch, seq_len)
    out = _out_mlp(mix, h, mlstm_w_out[0].astype(BF16), mlp_norm[1],
                   mlp_w1[1].astype(BF16), mlp_w2[1].astype(BF16), final_norm, final_norm=True)
    return out.reshape(batch, seq_len, d)
```

```python
import functools
import math

import numpy as np
import jax
import jax.numpy as jnp
from jax import lax
from jax.experimental import pallas as pl
from jax.experimental.pallas import tpu as pltpu

F32 = jnp.float32
BF16 = jnp.bfloat16

EPS = 1e-6
NEG = -1e30

DA_HEADS = 8
DA_HEAD_DIM = 64
DA_V_DIM = 2 * DA_HEAD_DIM
REL_BUCKETS = 32
REL_MAX_DIST = 128
DA_VT_ROWS = DA_V_DIM + 16
LOG2E = math.log2(math.e)

ML_HEADS = 4
ML_QK_DIM = 128
ML_V_DIM = 256
CONV_WIDTH = 4
ML_GATE_PAD = 128
ML_VT_ROWS = ML_V_DIM + 16

LANES = 128
BF16_SUBLANES = 16
VMEM_LIMIT = 56 * 1024 * 1024

ROW_TILE = 512
ATTN_TILE = 256
ML_CHUNK = 256
ML_PROJ_CHUNK = 256
ML_PROJ_AHEAD = 2
ATTN_PIPE_DEPTH = 4
ATTN_RING = 4
ATTN_TILES_PER_ITER = 2
HALO = BF16_SUBLANES


def _params(*sem):
    return pltpu.CompilerParams(dimension_semantics=sem, vmem_limit_bytes=VMEM_LIMIT)


def _const_spec(shape):
    nd = len(shape)
    return pl.BlockSpec(shape, lambda *_: (0,) * nd)


def _rmsnorm_rows(x, g):
    return x * lax.rsqrt(jnp.mean(x * x, axis=-1, keepdims=True) + EPS) * g


def _attn_proj_kernel(x_ref, g_ref, w_ref, qt_ref, k_ref, vt_ref, *, n_chunk, width, q_scale):
    xn = _rmsnorm_rows(x_ref[...], g_ref[...]).astype(BF16)
    tm = x_ref.shape[0]
    per_part = width // n_chunk
    heads_per_chunk = n_chunk // DA_V_DIM
    ones = jnp.ones((DA_VT_ROWS - DA_V_DIM, tm), BF16)
    for c in range(3 * per_part):
        y = jnp.dot(xn, w_ref[:, c * n_chunk:(c + 1) * n_chunk], preferred_element_type=F32)
        part, pc = divmod(c, per_part)
        if part == 0:
            qt_ref[0, pc * n_chunk:(pc + 1) * n_chunk, :] = jnp.transpose(y * q_scale).astype(BF16)
        elif part == 1:
            k_ref[:, pc * n_chunk:(pc + 1) * n_chunk] = y.astype(BF16)
        else:
            y_t = jnp.transpose(y).astype(BF16)
            for hh in range(heads_per_chunk):
                r0 = (pc * heads_per_chunk + hh) * DA_VT_ROWS
                vt_ref[0, r0:r0 + DA_V_DIM, :] = y_t[hh * DA_V_DIM:(hh + 1) * DA_V_DIM]
                vt_ref[0, r0 + DA_V_DIM:r0 + DA_VT_ROWS, :] = ones


def _attn_proj(x2, g, w_bf16, batch, seq_len):
    t, d = x2.shape
    width = w_bf16.shape[1] // 3
    n_chunk = 512
    tm = ROW_TILE
    assert seq_len % tm == 0 and width % n_chunk == 0 and width == DA_HEADS * DA_V_DIM
    tiles = seq_len // tm
    kern = functools.partial(_attn_proj_kernel, n_chunk=n_chunk, width=width,
                             q_scale=DA_HEAD_DIM ** -0.5 * LOG2E)
    return pl.pallas_call(
        kern,
        grid=(t // tm,),
        in_specs=[pl.BlockSpec((tm, d), lambda i: (i, 0)),
                  _const_spec((1, d)),
                  _const_spec((d, 3 * width))],
        out_specs=(pl.BlockSpec((1, width, tm), lambda i: (i // tiles, 0, i % tiles)),
                   pl.BlockSpec((tm, width), lambda i: (i, 0)),
                   pl.BlockSpec((1, DA_HEADS * DA_VT_ROWS, tm), lambda i: (i // tiles, 0, i % tiles))),
        out_shape=(jax.ShapeDtypeStruct((batch, width, seq_len), BF16),
                   jax.ShapeDtypeStruct((t, width), BF16),
                   jax.ShapeDtypeStruct((batch, DA_HEADS * DA_VT_ROWS, seq_len), BF16)),
        compiler_params=_params("parallel"),
        name="attn_proj",
    )(x2, g.reshape(1, d), w_bf16)


def _bucket_band(tile):
    r = np.arange(3 * tile)[:, None]
    i = np.arange(tile)[None, :]
    dist = i + 2 * tile - r
    max_exact = REL_BUCKETS // 2
    d = np.maximum(dist, 1).astype(np.float32)
    large = max_exact + (np.log(d / np.float32(max_exact)) / np.float32(math.log(REL_MAX_DIST / max_exact))
                         * np.float32(REL_BUCKETS - max_exact)).astype(np.int32)
    large = np.minimum(large, REL_BUCKETS - 1)
    bucket = np.where(dist < max_exact, dist, large)
    bucket = np.where(dist < 0, -1, bucket).astype(np.int32)
    assert (bucket[:tile] == REL_BUCKETS - 1).all()
    return bucket


def _bias_band_kernel(rel_ref, bucket_ref, o_ref):
    h = pl.program_id(0)
    bucket = bucket_ref[...]
    acc = jnp.full(bucket.shape, NEG, F32)
    for b in range(REL_BUCKETS):
        acc = jnp.where(bucket == b, rel_ref[b, h] * LOG2E, acc)
    o_ref[0] = acc


def _bias_band(rel_bias, tile):
    bucket = jnp.asarray(_bucket_band(tile))
    return pl.pallas_call(
        _bias_band_kernel,
        grid=(DA_HEADS,),
        in_specs=[pl.BlockSpec(memory_space=pltpu.SMEM),
                  _const_spec((3 * tile, tile))],
        out_specs=pl.BlockSpec((1, 3 * tile, tile), lambda h: (h, 0, 0)),
        out_shape=jax.ShapeDtypeStruct((DA_HEADS, 3 * tile, tile), F32),
        compiler_params=_params("arbitrary"),
        name="bias_band",
    )(rel_bias, bucket)


def _diff_attn_kernel(lam_ref, qt_ref, k_ref, vt_ref, band_ref, sub_ref, o_ref,
                      qz_ref, m_ref, acc_ref, sb_ref, cm_ref, *, tile, n_tiles, lambda_init):
    i = pl.program_id(1)
    head_cols = [slice(h * DA_V_DIM, (h + 1) * DA_V_DIM) for h in range(DA_HEADS)]
    depth = ATTN_PIPE_DEPTH
    slots = sb_ref.shape[0]

    sub = lax.broadcasted_iota(jnp.int32, (DA_V_DIM, tile), 0)
    first_map = (sub < DA_HEAD_DIM).astype(F32)
    for h in range(DA_HEADS):
        q_t = qt_ref[0, head_cols[h], :].astype(F32)
        qz_ref[h] = jnp.concatenate([q_t * first_map, q_t * (1.0 - first_map)], axis=1).astype(BF16)
    m_ref[...] = jnp.full(m_ref.shape, NEG, F32)
    acc_ref[...] = jnp.zeros(acc_ref.shape, F32)

    def key_rows(j):
        return pl.ds(pl.multiple_of(j * tile, tile), tile)

    def scores(j, h):
        return jnp.dot(k_ref[key_rows(j), head_cols[h]], qz_ref[h], preferred_element_type=F32)

    def drain(j, h, s_raw):
        kind = jnp.clip(j - (i - 2), 0, 2)
        bias = band_ref[h, pl.ds(pl.multiple_of(kind * tile, tile), tile), :]
        sb = s_raw + jnp.concatenate([bias, bias], axis=1)
        sb_ref[h % slots] = sb
        cm_ref[h % slots] = jnp.max(sb.reshape(tile // 8, 8, 2 * tile), axis=0)

    def consume(j, h):
        m = m_ref[h]
        m_new = jnp.maximum(m, jnp.max(cm_ref[h % slots], axis=0, keepdims=True))
        m_ref[h] = m_new
        p = jnp.exp2((sb_ref[h % slots] - m_new).astype(BF16))
        vt_blk = vt_ref[0, h * DA_VT_ROWS:(h + 1) * DA_VT_ROWS, key_rows(j)]
        return jnp.dot(vt_blk, p, preferred_element_type=F32), jnp.exp2(m - m_new)

    def rescale_add(h, pv, alpha):
        acc_ref[h] = alpha * acc_ref[h] + pv

    def item(j, n):
        return jnp.minimum(j + n // DA_HEADS, n_tiles - 1), n % DA_HEADS

    for n in range(depth):
        drain(0, n, scores(0, n))

    def run_tiles(j0, tiles):
        issued = None
        pending = None
        for n in range(tiles * DA_HEADS):
            jn, hn = item(j0, n + depth)
            nxt = (jn, hn, scores(jn, hn))
            pv, alpha = consume(j0 + n // DA_HEADS, n % DA_HEADS)
            if issued is not None:
                drain(*issued)
            if pending is not None:
                rescale_add(*pending)
            issued, pending = nxt, (n % DA_HEADS, pv, alpha)
        drain(*issued)
        rescale_add(*pending)

    n_key_tiles = i + 1
    per_iter = ATTN_TILES_PER_ITER

    def multi_tile_body(jj, carry):
        run_tiles(jj * per_iter, per_iter)
        return carry

    lax.fori_loop(0, n_key_tiles // per_iter, multi_tile_body, 0)
    for r in range(1, per_iter):
        @pl.when(n_key_tiles % per_iter == r)
        def _(r=r):
            run_tiles(n_key_tiles - r, r)

    lp = lam_ref[...]
    lam = (jnp.exp(jnp.sum(lp[0:1] * lp[1:2], axis=-1, keepdims=True))
           - jnp.exp(jnp.sum(lp[2:3] * lp[3:4], axis=-1, keepdims=True)) + lambda_init)
    sub_w = sub_ref[...] * (1.0 - lambda_init)
    for h in range(DA_HEADS):
        inv_l = 1.0 / acc_ref[h, DA_V_DIM:DA_V_DIM + 1, :]
        o = (acc_ref[h, 0:DA_V_DIM, :tile] * inv_l[:, :tile]
             - acc_ref[h, 0:DA_V_DIM, tile:] * (lam * inv_l[:, tile:]))
        o = o * lax.rsqrt(jnp.mean(o * o, axis=0, keepdims=True) + EPS)
        o_ref[0, :, head_cols[h]] = (jnp.transpose(o) * sub_w).astype(o_ref.dtype)


def _diff_attn(q_t, k, v_t, band, lam_params, subln, lambda_init):
    b, width, s = q_t.shape
    tile = ATTN_TILE
    assert s % tile == 0 and width == DA_HEADS * DA_V_DIM
    kern = functools.partial(_diff_attn_kernel, tile=tile, n_tiles=s // tile, lambda_init=lambda_init)
    return pl.pallas_call(
        kern,
        grid=(b, s // tile),
        in_specs=[_const_spec(lam_params.shape),
                  pl.BlockSpec((1, width, tile), lambda bi, i: (bi, 0, i)),
                  pl.BlockSpec((s, width), lambda bi, i: (bi, 0)),
                  pl.BlockSpec((1, DA_HEADS * DA_VT_ROWS, s), lambda bi, i: (bi, 0, 0)),
                  _const_spec(band.shape),
                  _const_spec((1, DA_V_DIM))],
        out_specs=pl.BlockSpec((1, tile, width), lambda bi, i: (bi, i, 0)),
        out_shape=jax.ShapeDtypeStruct((b, s, width), BF16),
        scratch_shapes=[pltpu.VMEM((DA_HEADS, DA_V_DIM, 2 * tile), BF16),
                        pltpu.VMEM((DA_HEADS, 1, 2 * tile), F32),
                        pltpu.VMEM((DA_HEADS, DA_VT_ROWS, 2 * tile), F32),
                        pltpu.VMEM((ATTN_RING, tile, 2 * tile), F32),
                        pltpu.VMEM((ATTN_RING, 8, 2 * tile), F32)],
        compiler_params=_params("parallel", "arbitrary"),
        name="diff_attn",
    )(lam_params, q_t, k, v_t, band, subln.reshape(1, DA_V_DIM))


def _out_mlp_kernel(a_ref, h_ref, wo_ref, g_ref, w1_ref, w2_ref, gf_ref, o_ref, u_ref,
                    *, ff_chunk, final_norm):
    h1 = h_ref[...] + jnp.dot(a_ref[...], wo_ref[...], preferred_element_type=F32)
    xn = _rmsnorm_rows(h1, g_ref[...]).astype(BF16)
    d_ff = w1_ref.shape[1]
    for c in range(d_ff // ff_chunk):
        sl = slice(c * ff_chunk, (c + 1) * ff_chunk)
        u = jnp.maximum(jnp.dot(xn, w1_ref[:, sl], preferred_element_type=F32), 0.0)
        u_ref[:, sl] = (u * u).astype(BF16)
    h2 = h1 + jnp.dot(u_ref[...], w2_ref[...], preferred_element_type=F32)
    if final_norm:
        h2 = _rmsnorm_rows(h2, gf_ref[...])
    o_ref[...] = h2


def _out_mlp(a2, h2, wo, g, w1, w2, gf, final_norm):
    t, d = h2.shape
    d_ff = w1.shape[1]
    kern = functools.partial(_out_mlp_kernel, ff_chunk=1024, final_norm=final_norm)
    single = pl.Buffered(1)
    return pl.pallas_call(
        kern,
        grid=(t // ROW_TILE,),
        in_specs=[pl.BlockSpec((ROW_TILE, a2.shape[1]), lambda i: (i, 0)),
                  pl.BlockSpec((ROW_TILE, d), lambda i: (i, 0)),
                  pl.BlockSpec(wo.shape, lambda i: (0, 0), pipeline_mode=single),
                  _const_spec((1, d)),
                  pl.BlockSpec(w1.shape, lambda i: (0, 0), pipeline_mode=single),
                  pl.BlockSpec(w2.shape, lambda i: (0, 0), pipeline_mode=single),
                  _const_spec((1, d))],
        out_specs=pl.BlockSpec((ROW_TILE, d), lambda i: (i, 0)),
        out_shape=jax.ShapeDtypeStruct((t, d), F32),
        scratch_shapes=[pltpu.VMEM((ROW_TILE, d_ff), BF16)],
        compiler_params=_params("parallel"),
        name="out_mlp_final" if final_norm else "out_mlp",
    )(a2, h2, wo, g.reshape(1, d), w1, w2, gf.reshape(1, d))


def _log_sigmoid(x):
    return -(jnp.maximum(-x, 0.0) + jnp.log(1.0 + jnp.exp(-jnp.abs(x))))


def _ml_proj_kernel(x_ref, xh_ref, g_ref, w_ref, cw_ref, cb_ref, gb_ref,
                    qt_ref, k_ref, vt_ref, og_ref, gcol_ref, grow_ref, cs_ref,
                    *, tiles_per_seq, q_scale, qk_w, v_w):
    i = pl.program_id(0)
    tm = x_ref.shape[0]
    nc = ML_PROJ_CHUNK
    g = g_ref[...]
    xn = _rmsnorm_rows(x_ref[...], g).astype(BF16)
    xh = _rmsnorm_rows(xh_ref[...], g).astype(BF16)
    seq_start = (i % tiles_per_seq) == 0
    ones = jnp.ones((ML_VT_ROWS - ML_V_DIM, tm), BF16)
    n_chunks = (qk_w + 2 * v_w) // nc

    def matmul(c):
        if c == n_chunks:
            return jnp.dot(xn, w_ref[:, c * nc:c * nc + ML_GATE_PAD], preferred_element_type=F32)
        return jnp.dot(xn, w_ref[:, c * nc:(c + 1) * nc], preferred_element_type=F32)

    def epilogue(c, y):
        col0 = c * nc
        cols = slice(col0, col0 + nc)
        if col0 < qk_w:
            slot = c % 2
            halo = jnp.dot(xh, w_ref[:, cols], preferred_element_type=F32)
            cs_ref[slot, 0:HALO, :] = jnp.where(seq_start, 0.0, halo)
            cs_ref[slot, HALO:HALO + tm, :] = y
            conv = cb_ref[:, cols]
            for j in range(CONV_WIDTH):
                off = HALO - (CONV_WIDTH - 1) + j
                conv = conv + cw_ref[j:j + 1, cols] * cs_ref[slot, off:off + tm, :]
            act = conv * jax.nn.sigmoid(conv)
            if col0 < qk_w // 2:
                qt_ref[0, cols, :] = jnp.transpose(act * q_scale).astype(BF16)
            else:
                k_ref[:, col0 - qk_w // 2:col0 - qk_w // 2 + nc] = act.astype(BF16)
        elif col0 < qk_w + v_w:
            r0 = ((col0 - qk_w) // ML_V_DIM) * ML_VT_ROWS
            vt_ref[0, r0:r0 + ML_V_DIM, :] = jnp.transpose(y).astype(BF16)
            vt_ref[0, r0 + ML_V_DIM:r0 + ML_VT_ROWS, :] = ones
        else:
            o0 = col0 - qk_w - v_w
            og_ref[:, o0:o0 + nc] = jax.nn.sigmoid(y).astype(BF16)

    qk_chunks = list(range(qk_w // nc))
    light = list(range(qk_w // nc, n_chunks))
    order = []
    while qk_chunks or light:
        order += light[:1] + qk_chunks[:1] + light[1:2]
        qk_chunks, light = qk_chunks[1:], light[2:]
    order.append(n_chunks)
    results = [matmul(c) for c in order[:ML_PROJ_AHEAD]]
    for pos, c in enumerate(order[:-1]):
        if pos + ML_PROJ_AHEAD < len(order):
            results.append(matmul(order[pos + ML_PROJ_AHEAD]))
        epilogue(c, results[pos])
    y = results[-1]
    gates = y + gb_ref[...]
    lane = lax.broadcasted_iota(jnp.int32, gates.shape, 1)
    gates = jnp.where(lane < ML_HEADS, gates, _log_sigmoid(gates))
    gcol_ref[...] = gates
    grow_ref[...] = jnp.transpose(gates)[0:grow_ref.shape[0], :]


def _ml_proj(x2, xnorm_g, w_all, conv_w, conv_b, gate_b, batch, seq_len):
    t, d = x2.shape
    tm = ROW_TILE
    qk_w = 2 * ML_HEADS * ML_QK_DIM
    v_w = ML_HEADS * ML_V_DIM
    assert seq_len % tm == 0 and tm % HALO == 0 and ML_PROJ_CHUNK == ML_V_DIM
    assert w_all.shape[1] == qk_w + 2 * v_w + ML_GATE_PAD
    tiles = seq_len // tm
    kern = functools.partial(_ml_proj_kernel, tiles_per_seq=tiles, q_scale=ML_QK_DIM ** -0.5,
                             qk_w=qk_w, v_w=v_w)
    halo_blocks = tm // HALO
    out_shapes = (jax.ShapeDtypeStruct((batch, qk_w // 2, seq_len), BF16),
                  jax.ShapeDtypeStruct((t, qk_w // 2), BF16),
                  jax.ShapeDtypeStruct((batch, ML_HEADS * ML_VT_ROWS, seq_len), BF16),
                  jax.ShapeDtypeStruct((t, v_w), BF16),
                  jax.ShapeDtypeStruct((t, ML_GATE_PAD), F32),
                  jax.ShapeDtypeStruct((2 * ML_HEADS, t), F32))
    return pl.pallas_call(
        kern,
        grid=(t // tm,),
        in_specs=[pl.BlockSpec((tm, d), lambda i: (i, 0)),
                  pl.BlockSpec((HALO, d), lambda i: (jnp.maximum(i * halo_blocks - 1, 0), 0)),
                  _const_spec((1, d)),
                  pl.BlockSpec(w_all.shape, lambda i: (0, 0), pipeline_mode=pl.Buffered(1)),
                  _const_spec(conv_w.shape),
                  _const_spec((1, qk_w)),
                  _const_spec((1, ML_GATE_PAD))],
        out_specs=(pl.BlockSpec((1, qk_w // 2, tm), lambda i: (i // tiles, 0, i % tiles)),
                   pl.BlockSpec((tm, qk_w // 2), lambda i: (i, 0)),
                   pl.BlockSpec((1, ML_HEADS * ML_VT_ROWS, tm), lambda i: (i // tiles, 0, i % tiles)),
                   pl.BlockSpec((tm, v_w), lambda i: (i, 0)),
                   pl.BlockSpec((tm, ML_GATE_PAD), lambda i: (i, 0)),
                   pl.BlockSpec((2 * ML_HEADS, tm), lambda i: (0, i))),
        out_shape=out_shapes,
        scratch_shapes=[pltpu.VMEM((2, HALO + tm, ML_PROJ_CHUNK), F32)],
        compiler_params=_params("parallel"),
        name="mlstm_proj",
    )(x2, x2, xnorm_g.reshape(1, d), w_all, conv_w, conv_b.reshape(1, qk_w),
      gate_b.reshape(1, ML_GATE_PAD))


def _mlstm_kernel(qt_ref, k_ref, vt_ref, og_ref, gcol_ref, grow_ref, hn_ref, o_ref, s_ref, m_ref):
    @pl.when(pl.program_id(1) == 0)
    def _():
        s_ref[...] = jnp.zeros_like(s_ref)
        m_ref[...] = jnp.zeros_like(m_ref)

    L = k_ref.shape[0]
    row = lax.broadcasted_iota(jnp.int32, (L, L), 0)
    col = lax.broadcasted_iota(jnp.int32, (L, L), 1)
    key_le_query = row <= col
    gcol = gcol_ref[...]
    grow = grow_ref[...]

    def qk_blocks(h):
        return (qt_ref[0, h * ML_QK_DIM:(h + 1) * ML_QK_DIM, :],
                k_ref[:, h * ML_QK_DIM:(h + 1) * ML_QK_DIM])

    def matmuls(h):
        q_t, kh = qk_blocks(h)
        s_t = jnp.dot(kh, q_t, preferred_element_type=F32)
        sq = jnp.dot(s_ref[h].astype(BF16), q_t, preferred_element_type=F32)
        return s_t, sq

    def head(h, s_t, sq):
        _, kh = qk_blocks(h)
        vcols = slice(h * ML_V_DIM, (h + 1) * ML_V_DIM)
        v_t = vt_ref[0, h * ML_VT_ROWS:(h + 1) * ML_VT_ROWS, :]
        ig_r = grow[h:h + 1, :]
        lf_r = grow[ML_HEADS + h:ML_HEADS + h + 1, :]
        ig_c = gcol[:, h:h + 1]
        lf_c = gcol[:, ML_HEADS + h:ML_HEADS + h + 1]
        m_prev = m_ref[h]

        b_r = jnp.sum(jnp.where(key_le_query, lf_c, 0.0), axis=0, keepdims=True)
        b_c = jnp.sum(jnp.where(key_le_query, 0.0, lf_r), axis=1, keepdims=True) + lf_c
        b_last = jnp.sum(lf_r, axis=1, keepdims=True)

        d_t = jnp.where(key_le_query, b_r - (b_c - ig_c), NEG)
        m_inter = b_r + m_prev
        m_j = jnp.maximum(m_inter, jnp.max(d_t, axis=0, keepdims=True))
        sc_t = s_t * jnp.exp(d_t - m_j)
        inter = jnp.exp(m_inter - m_j)
        tot = jnp.dot(v_t, sc_t.astype(BF16), preferred_element_type=F32) + inter * sq
        den = tot[ML_V_DIM:ML_V_DIM + 1, :]
        h_t = tot[0:ML_V_DIM, :] * (1.0 / jnp.maximum(jnp.abs(den), jnp.exp(-m_j)))
        y_t = h_t * lax.rsqrt(jnp.mean(h_t * h_t, axis=0, keepdims=True) + EPS)
        y = jnp.transpose(y_t) * hn_ref[:, vcols]
        o_ref[:, vcols] = (og_ref[:, vcols].astype(F32) * y).astype(o_ref.dtype)

        a_r = b_last - b_r + ig_r
        m_new = jnp.maximum(b_last + m_prev, jnp.max(a_r, axis=1, keepdims=True))
        w_r = jnp.exp(a_r - m_new)
        decay = jnp.exp(b_last + m_prev - m_new)
        wv_t = (v_t.astype(F32) * w_r).astype(BF16)
        s_ref[h] = decay * s_ref[h] + jnp.dot(wv_t, kh, preferred_element_type=F32)
        m_ref[h] = m_new

    ahead = 2
    ready = [matmuls(h) for h in range(ahead)]
    for h in range(ML_HEADS):
        if h + ahead < ML_HEADS:
            ready.append(matmuls(h + ahead))
        head(h, *ready[h])


def _mlstm(q_t, k, v_t, og, gcol, grow, head_norm, batch, seq_len):
    L = ML_CHUNK
    assert seq_len % L == 0
    nc = seq_len // L
    v_w = og.shape[1]
    row_map = lambda b, c: (b * nc + c, 0)
    lane_map = lambda b, c: (b, 0, c)
    return pl.pallas_call(
        _mlstm_kernel,
        grid=(batch, nc),
        in_specs=[pl.BlockSpec((1, q_t.shape[1], L), lane_map),
                  pl.BlockSpec((L, k.shape[1]), row_map),
                  pl.BlockSpec((1, v_t.shape[1], L), lane_map),
                  pl.BlockSpec((L, v_w), row_map),
                  pl.BlockSpec((L, ML_GATE_PAD), row_map),
                  pl.BlockSpec((2 * ML_HEADS, L), lambda b, c: (0, b * nc + c)),
                  _const_spec((1, v_w))],
        out_specs=pl.BlockSpec((L, v_w), row_map),
        out_shape=jax.ShapeDtypeStruct((batch * seq_len, v_w), BF16),
        scratch_shapes=[pltpu.VMEM((ML_HEADS, ML_VT_ROWS, ML_QK_DIM), F32),
                        pltpu.VMEM((ML_HEADS, 1, 1), F32)],
        compiler_params=_params("parallel", "arbitrary"),
        name="mlstm_chunk",
    )(q_t, k, v_t, og, gcol, grow, head_norm.reshape(1, v_w))


def kernel(x, rel_bias, attn_norm, attn_w_in, attn_lambda_q1, attn_lambda_k1, attn_lambda_q2,
           attn_lambda_k2, attn_subln, attn_w_out, mlstm_norm, mlstm_w_in, mlstm_b_i, mlstm_b_f,
           mlstm_conv_w, mlstm_conv_b, mlstm_head_norm, mlstm_w_out, mlp_norm, mlp_w1, mlp_w2,
           final_norm):
    batch, seq_len, d = x.shape
    t = batch * seq_len
    h = x.reshape(t, d)

    lambda_init = 0.8 - 0.6 * math.exp(-0.3 * 0)
    q_t, k, v_t = _attn_proj(h, attn_norm[0], attn_w_in[0].astype(BF16), batch, seq_len)
    band = _bias_band(rel_bias, ATTN_TILE)
    lam_params = jnp.stack([attn_lambda_q1[0], attn_lambda_k1[0],
                            attn_lambda_q2[0], attn_lambda_k2[0]]).astype(F32)
    attn = _diff_attn(q_t, k, v_t, band, lam_params, attn_subln[0], lambda_init)
    h = _out_mlp(attn.reshape(t, -1), h, attn_w_out[0].astype(BF16), mlp_norm[0],
                 mlp_w1[0].astype(BF16), mlp_w2[0].astype(BF16), final_norm, final_norm=False)

    n_gates = 2 * ML_HEADS
    w_all = jnp.pad(mlstm_w_in[0], ((0, 0), (0, ML_GATE_PAD - n_gates))).astype(BF16)
    gate_b = jnp.pad(jnp.concatenate([mlstm_b_i[0], mlstm_b_f[0]]).astype(F32),
                     (0, ML_GATE_PAD - n_gates))
    q_t, k, v_t, og, gcol, grow = _ml_proj(h, mlstm_norm[0], w_all, mlstm_conv_w[0],
                                           mlstm_conv_b[0], gate_b, batch, seq_len)
    mix = _mlstm(q_t, k, v_t, og, gcol, grow, mlstm_head_norm[0], batch, seq_len)
    out = _out_mlp(mix, h, mlstm_w_out[0].astype(BF16), mlp_norm[1],
                   mlp_w1[1].astype(BF16), mlp_w2[1].astype(BF16), final_norm, final_norm=True)
    return out.reshape(batch, seq_len, d)
```

```python
import functools
import math

import numpy as np
import jax
import jax.numpy as jnp
from jax import lax
from jax.experimental import pallas as pl
from jax.experimental.pallas import tpu as pltpu

F32 = jnp.float32
BF16 = jnp.bfloat16

EPS = 1e-6
NEG = -1e30

DA_HEADS = 8
DA_HEAD_DIM = 64
DA_V_DIM = 2 * DA_HEAD_DIM
REL_BUCKETS = 32
REL_MAX_DIST = 128
DA_VT_ROWS = DA_V_DIM + 16
LOG2E = math.log2(math.e)

ML_HEADS = 4
ML_QK_DIM = 128
ML_V_DIM = 256
CONV_WIDTH = 4
ML_GATE_PAD = 128
ML_VT_ROWS = ML_V_DIM + 16

LANES = 128
BF16_SUBLANES = 16
VMEM_LIMIT = 56 * 1024 * 1024

ROW_TILE = 512
ATTN_TILE = 256
ML_CHUNK = 256
ML_PROJ_CHUNK = 256
ML_PROJ_AHEAD = 2
ATTN_PIPE_DEPTH = 4
ATTN_RING = 4
ATTN_TILES_PER_ITER = 2
HALO = BF16_SUBLANES


def _params(*sem):
    return pltpu.CompilerParams(dimension_semantics=sem, vmem_limit_bytes=VMEM_LIMIT)


def _const_spec(shape):
    nd = len(shape)
    return pl.BlockSpec(shape, lambda *_: (0,) * nd)


def _rmsnorm_rows(x, g):
    return x * lax.rsqrt(jnp.mean(x * x, axis=-1, keepdims=True) + EPS) * g


def _attn_proj_kernel(x_ref, g_ref, w_ref, qt_ref, k_ref, vt_ref, *, n_chunk, width, q_scale):
    xn = _rmsnorm_rows(x_ref[...], g_ref[...]).astype(BF16)
    tm = x_ref.shape[0]
    per_part = width // n_chunk
    heads_per_chunk = n_chunk // DA_V_DIM
    ones = jnp.ones((DA_VT_ROWS - DA_V_DIM, tm), BF16)
    for c in range(3 * per_part):
        y = jnp.dot(xn, w_ref[:, c * n_chunk:(c + 1) * n_chunk], preferred_element_type=F32)
        part, pc = divmod(c, per_part)
        if part == 0:
            qt_ref[0, pc * n_chunk:(pc + 1) * n_chunk, :] = jnp.transpose(y * q_scale).astype(BF16)
        elif part == 1:
            k_ref[:, pc * n_chunk:(pc + 1) * n_chunk] = y.astype(BF16)
        else:
            y_t = jnp.transpose(y).astype(BF16)
            for hh in range(heads_per_chunk):
                r0 = (pc * heads_per_chunk + hh) * DA_VT_ROWS
                vt_ref[0, r0:r0 + DA_V_DIM, :] = y_t[hh * DA_V_DIM:(hh + 1) * DA_V_DIM]
                vt_ref[0, r0 + DA_V_DIM:r0 + DA_VT_ROWS, :] = ones


def _attn_proj(x2, g, w_bf16, batch, seq_len):
    t, d = x2.shape
    width = w_bf16.shape[1] // 3
    n_chunk = 512
    tm = ROW_TILE
    assert seq_len % tm == 0 and width % n_chunk == 0 and width == DA_HEADS * DA_V_DIM
    tiles = seq_len // tm
    kern = functools.partial(_attn_proj_kernel, n_chunk=n_chunk, width=width,
                             q_scale=DA_HEAD_DIM ** -0.5 * LOG2E)
    return pl.pallas_call(
        kern,
        grid=(t // tm,),
        in_specs=[pl.BlockSpec((tm, d), lambda i: (i, 0)),
                  _const_spec((1, d)),
                  _const_spec((d, 3 * width))],
        out_specs=(pl.BlockSpec((1, width, tm), lambda i: (i // tiles, 0, i % tiles)),
                   pl.BlockSpec((tm, width), lambda i: (i, 0)),
                   pl.BlockSpec((1, DA_HEADS * DA_VT_ROWS, tm), lambda i: (i // tiles, 0, i % tiles))),
        out_shape=(jax.ShapeDtypeStruct((batch, width, seq_len), BF16),
                   jax.ShapeDtypeStruct((t, width), BF16),
                   jax.ShapeDtypeStruct((batch, DA_HEADS * DA_VT_ROWS, seq_len), BF16)),
        compiler_params=_params("parallel"),
        name="attn_proj",
    )(x2, g.reshape(1, d), w_bf16)


def _bucket_band(tile):
    r = np.arange(3 * tile)[:, None]
    i = np.arange(tile)[None, :]
    dist = i + 2 * tile - r
    max_exact = REL_BUCKETS // 2
    d = np.maximum(dist, 1).astype(np.float32)
    large = max_exact + (np.log(d / np.float32(max_exact)) / np.float32(math.log(REL_MAX_DIST / max_exact))
                         * np.float32(REL_BUCKETS - max_exact)).astype(np.int32)
    large = np.minimum(large, REL_BUCKETS - 1)
    bucket = np.where(dist < max_exact, dist, large)
    bucket = np.where(dist < 0, -1, bucket).astype(np.int32)
    assert (bucket[:tile] == REL_BUCKETS - 1).all()
    return bucket


def _bias_band_kernel(rel_ref, bucket_ref, o_ref):
    h = pl.program_id(0)
    bucket = bucket_ref[...]
    acc = jnp.full(bucket.shape, NEG, F32)
    for b in range(REL_BUCKETS):
        acc = jnp.where(bucket == b, rel_ref[b, h] * LOG2E, acc)
    o_ref[0] = acc


def _bias_band(rel_bias, tile):
    bucket = jnp.asarray(_bucket_band(tile))
    return pl.pallas_call(
        _bias_band_kernel,
        grid=(DA_HEADS,),
        in_specs=[pl.BlockSpec(memory_space=pltpu.SMEM),
                  _const_spec((3 * tile, tile))],
        out_specs=pl.BlockSpec((1, 3 * tile, tile), lambda h: (h, 0, 0)),
        out_shape=jax.ShapeDtypeStruct((DA_HEADS, 3 * tile, tile), F32),
        compiler_params=_params("arbitrary"),
        name="bias_band",
    )(rel_bias, bucket)


def _diff_attn_kernel(lam_ref, qt_ref, k_ref, vt_ref, band_ref, sub_ref, o_ref,
                      qz_ref, m_ref, acc_ref, sb_ref, cm_ref, *, tile, n_tiles, lambda_init):
    head_cols = [slice(h * DA_V_DIM, (h + 1) * DA_V_DIM) for h in range(DA_HEADS)]
    depth = ATTN_PIPE_DEPTH
    slots = sb_ref.shape[0]
    per_iter = ATTN_TILES_PER_ITER

    def tile_span(j):
        return pl.ds(pl.multiple_of(j * tile, tile), tile)

    def scores(j, h):
        return jnp.dot(k_ref[tile_span(j), head_cols[h]], qz_ref[h], preferred_element_type=F32)

    def drain(i, j, h, s_raw):
        kind = jnp.clip(j - (i - 2), 0, 2)
        bias = band_ref[h, pl.ds(pl.multiple_of(kind * tile, tile), tile), :]
        sb = s_raw + jnp.concatenate([bias, bias], axis=1)
        sb_ref[h % slots] = sb
        cm_ref[h % slots] = jnp.max(sb.reshape(tile // 8, 8, 2 * tile), axis=0)

    def consume(j, h):
        m = m_ref[h]
        m_new = jnp.maximum(m, jnp.max(cm_ref[h % slots], axis=0, keepdims=True))
        m_ref[h] = m_new
        p = jnp.exp2((sb_ref[h % slots] - m_new).astype(BF16))
        vt_blk = vt_ref[0, h * DA_VT_ROWS:(h + 1) * DA_VT_ROWS, tile_span(j)]
        return jnp.dot(vt_blk, p, preferred_element_type=F32), jnp.exp2(m - m_new)

    def rescale_add(h, pv, alpha):
        acc_ref[h] = alpha * acc_ref[h] + pv

    def item(j, n):
        return jnp.minimum(j + n // DA_HEADS, n_tiles - 1), n % DA_HEADS

    def run_tiles(i, j0, tiles):
        issued = None
        pending = None
        for n in range(tiles * DA_HEADS):
            jn, hn = item(j0, n + depth)
            nxt = (i, jn, hn, scores(jn, hn))
            pv, alpha = consume(j0 + n // DA_HEADS, n % DA_HEADS)
            if issued is not None:
                drain(*issued)
            if pending is not None:
                rescale_add(*pending)
            issued, pending = nxt, (n % DA_HEADS, pv, alpha)
        drain(*issued)
        rescale_add(*pending)

    def open_query_tile(i):
        sub = lax.broadcasted_iota(jnp.int32, (DA_V_DIM, tile), 0)
        first_map = (sub < DA_HEAD_DIM).astype(F32)
        for h in range(DA_HEADS):
            q_t = qt_ref[0, head_cols[h], tile_span(i)].astype(F32)
            qz_ref[h] = jnp.concatenate([q_t * first_map, q_t * (1.0 - first_map)], axis=1).astype(BF16)
        for n in range(depth):
            drain(i, 0, n, scores(0, n))

    lp = lam_ref[...]
    lam = (jnp.exp(jnp.sum(lp[0:1] * lp[1:2], axis=-1, keepdims=True))
           - jnp.exp(jnp.sum(lp[2:3] * lp[3:4], axis=-1, keepdims=True)) + lambda_init)
    sub_w = sub_ref[...] * (1.0 - lambda_init)

    def query_tile(i, carry):
        m_ref[...] = jnp.full(m_ref.shape, NEG, F32)
        acc_ref[...] = jnp.zeros(acc_ref.shape, F32)

        n_key_tiles = i + 1

        def multi_tile_body(jj, c):
            run_tiles(i, jj * per_iter, per_iter)
            return c

        lax.fori_loop(0, n_key_tiles // per_iter, multi_tile_body, 0)
        for r in range(1, per_iter):
            @pl.when(n_key_tiles % per_iter == r)
            def _(r=r):
                run_tiles(i, n_key_tiles - r, r)

        for h in range(DA_HEADS):
            inv_l = 1.0 / acc_ref[h, DA_V_DIM:DA_V_DIM + 1, :]
            o = (acc_ref[h, 0:DA_V_DIM, :tile] * inv_l[:, :tile]
                 - acc_ref[h, 0:DA_V_DIM, tile:] * (lam * inv_l[:, tile:]))
            o = o * lax.rsqrt(jnp.mean(o * o, axis=0, keepdims=True) + EPS)
            o_ref[0, tile_span(i), head_cols[h]] = (jnp.transpose(o) * sub_w).astype(o_ref.dtype)
        open_query_tile(jnp.minimum(i + 1, n_tiles - 1))
        return carry

    open_query_tile(0)
    lax.fori_loop(0, n_tiles, query_tile, 0)


def _diff_attn(q_t, k, v_t, band, lam_params, subln, lambda_init):
    b, width, s = q_t.shape
    tile = ATTN_TILE
    assert s % tile == 0 and width == DA_HEADS * DA_V_DIM
    kern = functools.partial(_diff_attn_kernel, tile=tile, n_tiles=s // tile, lambda_init=lambda_init)
    return pl.pallas_call(
        kern,
        grid=(b,),
        in_specs=[_const_spec(lam_params.shape),
                  pl.BlockSpec((1, width, s), lambda bi: (bi, 0, 0)),
                  pl.BlockSpec((s, width), lambda bi: (bi, 0)),
                  pl.BlockSpec((1, DA_HEADS * DA_VT_ROWS, s), lambda bi: (bi, 0, 0)),
                  pl.BlockSpec(band.shape, lambda bi: (0, 0, 0), pipeline_mode=pl.Buffered(1)),
                  _const_spec((1, DA_V_DIM))],
        out_specs=pl.BlockSpec((1, s, width), lambda bi: (bi, 0, 0)),
        out_shape=jax.ShapeDtypeStruct((b, s, width), BF16),
        scratch_shapes=[pltpu.VMEM((DA_HEADS, DA_V_DIM, 2 * tile), BF16),
                        pltpu.VMEM((DA_HEADS, 1, 2 * tile), F32),
                        pltpu.VMEM((DA_HEADS, DA_VT_ROWS, 2 * tile), F32),
                        pltpu.VMEM((ATTN_RING, tile, 2 * tile), F32),
                        pltpu.VMEM((ATTN_RING, 8, 2 * tile), F32)],
        compiler_params=_params("parallel"),
        name="diff_attn",
    )(lam_params, q_t, k, v_t, band, subln.reshape(1, DA_V_DIM))


def _out_mlp_kernel(a_ref, h_ref, wo_ref, g_ref, w1_ref, w2_ref, gf_ref, o_ref, u_ref,
                    *, ff_chunk, final_norm):
    h1 = h_ref[...] + jnp.dot(a_ref[...], wo_ref[...], preferred_element_type=F32)
    xn = _rmsnorm_rows(h1, g_ref[...]).astype(BF16)
    d_ff = w1_ref.shape[1]
    for c in range(d_ff // ff_chunk):
        sl = slice(c * ff_chunk, (c + 1) * ff_chunk)
        u = jnp.maximum(jnp.dot(xn, w1_ref[:, sl], preferred_element_type=F32), 0.0)
        u_ref[:, sl] = (u * u).astype(BF16)
    h2 = h1 + jnp.dot(u_ref[...], w2_ref[...], preferred_element_type=F32)
    if final_norm:
        h2 = _rmsnorm_rows(h2, gf_ref[...])
    o_ref[...] = h2


def _out_mlp(a2, h2, wo, g, w1, w2, gf, final_norm):
    t, d = h2.shape
    d_ff = w1.shape[1]
    kern = functools.partial(_out_mlp_kernel, ff_chunk=1024, final_norm=final_norm)
    single = pl.Buffered(1)
    return pl.pallas_call(
        kern,
        grid=(t // ROW_TILE,),
        in_specs=[pl.BlockSpec((ROW_TILE, a2.shape[1]), lambda i: (i, 0)),
                  pl.BlockSpec((ROW_TILE, d), lambda i: (i, 0)),
                  pl.BlockSpec(wo.shape, lambda i: (0, 0), pipeline_mode=single),
                  _const_spec((1, d)),
                  pl.BlockSpec(w1.shape, lambda i: (0, 0), pipeline_mode=single),
                  pl.BlockSpec(w2.shape, lambda i: (0, 0), pipeline_mode=single),
                  _const_spec((1, d))],
        out_specs=pl.BlockSpec((ROW_TILE, d), lambda i: (i, 0)),
        out_shape=jax.ShapeDtypeStruct((t, d), F32),
        scratch_shapes=[pltpu.VMEM((ROW_TILE, d_ff), BF16)],
        compiler_params=_params("parallel"),
        name="out_mlp_final" if final_norm else "out_mlp",
    )(a2, h2, wo, g.reshape(1, d), w1, w2, gf.reshape(1, d))


def _log_sigmoid(x):
    return -(jnp.maximum(-x, 0.0) + jnp.log(1.0 + jnp.exp(-jnp.abs(x))))


def _ml_proj_kernel(x_ref, xh_ref, g_ref, w_ref, cw_ref, cb_ref, gb_ref,
                    qt_ref, k_ref, vt_ref, og_ref, gcol_ref, grow_ref, cs_ref,
                    *, tiles_per_seq, q_scale, qk_w, v_w):
    i = pl.program_id(0)
    tm = x_ref.shape[0]
    nc = ML_PROJ_CHUNK
    g = g_ref[...]
    xn = _rmsnorm_rows(x_ref[...], g).astype(BF16)
    xh = _rmsnorm_rows(xh_ref[...], g).astype(BF16)
    seq_start = (i % tiles_per_seq) == 0
    ones = jnp.ones((ML_VT_ROWS - ML_V_DIM, tm), BF16)
    n_chunks = (qk_w + 2 * v_w) // nc

    def matmul(c):
        if c == n_chunks:
            return jnp.dot(xn, w_ref[:, c * nc:c * nc + ML_GATE_PAD], preferred_element_type=F32)
        return jnp.dot(xn, w_ref[:, c * nc:(c + 1) * nc], preferred_element_type=F32)

    def epilogue(c, y):
        col0 = c * nc
        cols = slice(col0, col0 + nc)
        if col0 < qk_w:
            slot = c % 2
            halo = jnp.dot(xh, w_ref[:, cols], preferred_element_type=F32)
            cs_ref[slot, 0:HALO, :] = jnp.where(seq_start, 0.0, halo)
            cs_ref[slot, HALO:HALO + tm, :] = y
            conv = cb_ref[:, cols]
            for j in range(CONV_WIDTH):
                off = HALO - (CONV_WIDTH - 1) + j
                conv = conv + cw_ref[j:j + 1, cols] * cs_ref[slot, off:off + tm, :]
            act = conv * jax.nn.sigmoid(conv)
            if col0 < qk_w // 2:
                qt_ref[0, cols, :] = jnp.transpose(act * q_scale).astype(BF16)
            else:
                k_ref[:, col0 - qk_w // 2:col0 - qk_w // 2 + nc] = act.astype(BF16)
        elif col0 < qk_w + v_w:
            r0 = ((col0 - qk_w) // ML_V_DIM) * ML_VT_ROWS
            vt_ref[0, r0:r0 + ML_V_DIM, :] = jnp.transpose(y).astype(BF16)
            vt_ref[0, r0 + ML_V_DIM:r0 + ML_VT_ROWS, :] = ones
        else:
            o0 = col0 - qk_w - v_w
            og_ref[:, o0:o0 + nc] = jax.nn.sigmoid(y).astype(BF16)

    qk_chunks = list(range(qk_w // nc))
    light = list(range(qk_w // nc, n_chunks))
    order = []
    while qk_chunks or light:
        order += light[:1] + qk_chunks[:1] + light[1:2]
        qk_chunks, light = qk_chunks[1:], light[2:]
    order.append(n_chunks)
    results = [matmul(c) for c in order[:ML_PROJ_AHEAD]]
    for pos, c in enumerate(order[:-1]):
        if pos + ML_PROJ_AHEAD < len(order):
            results.append(matmul(order[pos + ML_PROJ_AHEAD]))
        epilogue(c, results[pos])
    y = results[-1]
    gates = y + gb_ref[...]
    lane = lax.broadcasted_iota(jnp.int32, gates.shape, 1)
    gates = jnp.where(lane < ML_HEADS, gates, _log_sigmoid(gates))
    gcol_ref[...] = gates
    grow_ref[...] = jnp.transpose(gates)[0:grow_ref.shape[0], :]


def _ml_proj(x2, xnorm_g, w_all, conv_w, conv_b, gate_b, batch, seq_len):
    t, d = x2.shape
    tm = ROW_TILE
    qk_w = 2 * ML_HEADS * ML_QK_DIM
    v_w = ML_HEADS * ML_V_DIM
    assert seq_len % tm == 0 and tm % HALO == 0 and ML_PROJ_CHUNK == ML_V_DIM
    assert w_all.shape[1] == qk_w + 2 * v_w + ML_GATE_PAD
    tiles = seq_len // tm
    kern = functools.partial(_ml_proj_kernel, tiles_per_seq=tiles, q_scale=ML_QK_DIM ** -0.5,
                             qk_w=qk_w, v_w=v_w)
    halo_blocks = tm // HALO
    out_shapes = (jax.ShapeDtypeStruct((batch, qk_w // 2, seq_len), BF16),
                  jax.ShapeDtypeStruct((t, qk_w // 2), BF16),
                  jax.ShapeDtypeStruct((batch, ML_HEADS * ML_VT_ROWS, seq_len), BF16),
                  jax.ShapeDtypeStruct((t, v_w), BF16),
                  jax.ShapeDtypeStruct((t, ML_GATE_PAD), F32),
                  jax.ShapeDtypeStruct((2 * ML_HEADS, t), F32))
    return pl.pallas_call(
        kern,
        grid=(t // tm,),
        in_specs=[pl.BlockSpec((tm, d), lambda i: (i, 0)),
                  pl.BlockSpec((HALO, d), lambda i: (jnp.maximum(i * halo_blocks - 1, 0), 0)),
                  _const_spec((1, d)),
                  pl.BlockSpec(w_all.shape, lambda i: (0, 0), pipeline_mode=pl.Buffered(1)),
                  _const_spec(conv_w.shape),
                  _const_spec((1, qk_w)),
                  _const_spec((1, ML_GATE_PAD))],
        out_specs=(pl.BlockSpec((1, qk_w // 2, tm), lambda i: (i // tiles, 0, i % tiles)),
                   pl.BlockSpec((tm, qk_w // 2), lambda i: (i, 0)),
                   pl.BlockSpec((1, ML_HEADS * ML_VT_ROWS, tm), lambda i: (i // tiles, 0, i % tiles)),
                   pl.BlockSpec((tm, v_w), lambda i: (i, 0)),
                   pl.BlockSpec((tm, ML_GATE_PAD), lambda i: (i, 0)),
                   pl.BlockSpec((2 * ML_HEADS, tm), lambda i: (0, i))),
        out_shape=out_shapes,
        scratch_shapes=[pltpu.VMEM((2, HALO + tm, ML_PROJ_CHUNK), F32)],
        compiler_params=_params("parallel"),
        name="mlstm_proj",
    )(x2, x2, xnorm_g.reshape(1, d), w_all, conv_w, conv_b.reshape(1, qk_w),
      gate_b.reshape(1, ML_GATE_PAD))


def _mlstm_kernel(qt_ref, k_ref, vt_ref, og_ref, gcol_ref, grow_ref, hn_ref, o_ref, s_ref, m_ref):
    @pl.when(pl.program_id(1) == 0)
    def _():
        s_ref[...] = jnp.zeros_like(s_ref)
        m_ref[...] = jnp.zeros_like(m_ref)

    L = k_ref.shape[0]
    row = lax.broadcasted_iota(jnp.int32, (L, L), 0)
    col = lax.broadcasted_iota(jnp.int32, (L, L), 1)
    key_le_query = row <= col
    gcol = gcol_ref[...]
    grow = grow_ref[...]

    def qk_blocks(h):
        return (qt_ref[0, h * ML_QK_DIM:(h + 1) * ML_QK_DIM, :],
                k_ref[:, h * ML_QK_DIM:(h + 1) * ML_QK_DIM])

    def matmuls(h):
        q_t, kh = qk_blocks(h)
        s_t = jnp.dot(kh, q_t, preferred_element_type=F32)
        sq = jnp.dot(s_ref[h].astype(BF16), q_t, preferred_element_type=F32)
        return s_t, sq

    def head(h, s_t, sq):
        _, kh = qk_blocks(h)
        vcols = slice(h * ML_V_DIM, (h + 1) * ML_V_DIM)
        v_t = vt_ref[0, h * ML_VT_ROWS:(h + 1) * ML_VT_ROWS, :]
        ig_r = grow[h:h + 1, :]
        lf_r = grow[ML_HEADS + h:ML_HEADS + h + 1, :]
        ig_c = gcol[:, h:h + 1]
        lf_c = gcol[:, ML_HEADS + h:ML_HEADS + h + 1]
        m_prev = m_ref[h]

        b_r = jnp.sum(jnp.where(key_le_query, lf_c, 0.0), axis=0, keepdims=True)
        b_c = jnp.sum(jnp.where(key_le_query, 0.0, lf_r), axis=1, keepdims=True) + lf_c
        b_last = jnp.sum(lf_r, axis=1, keepdims=True)

        d_t = jnp.where(key_le_query, b_r - (b_c - ig_c), NEG)
        m_inter = b_r + m_prev
        m_j = jnp.maximum(m_inter, jnp.max(d_t, axis=0, keepdims=True))
        sc_t = s_t * jnp.exp(d_t - m_j)
        inter = jnp.exp(m_inter - m_j)
        tot = jnp.dot(v_t, sc_t.astype(BF16), preferred_element_type=F32) + inter * sq
        den = tot[ML_V_DIM:ML_V_DIM + 1, :]
        h_t = tot[0:ML_V_DIM, :] * (1.0 / jnp.maximum(jnp.abs(den), jnp.exp(-m_j)))
        y_t = h_t * lax.rsqrt(jnp.mean(h_t * h_t, axis=0, keepdims=True) + EPS)
        y = jnp.transpose(y_t) * hn_ref[:, vcols]
        o_ref[:, vcols] = (og_ref[:, vcols].astype(F32) * y).astype(o_ref.dtype)

        a_r = b_last - b_r + ig_r
        m_new = jnp.maximum(b_last + m_prev, jnp.max(a_r, axis=1, keepdims=True))
        w_r = jnp.exp(a_r - m_new)
        decay = jnp.exp(b_last + m_prev - m_new)
        wv_t = (v_t.astype(F32) * w_r).astype(BF16)
        s_ref[h] = decay * s_ref[h] + jnp.dot(wv_t, kh, preferred_element_type=F32)
        m_ref[h] = m_new

    ahead = 2
    ready = [matmuls(h) for h in range(ahead)]
    for h in range(ML_HEADS):
        if h + ahead < ML_HEADS:
            ready.append(matmuls(h + ahead))
        head(h, *ready[h])


def _mlstm(q_t, k, v_t, og, gcol, grow, head_norm, batch, seq_len):
    L = ML_CHUNK
    assert seq_len % L == 0
    nc = seq_len // L
    v_w = og.shape[1]
    row_map = lambda b, c: (b * nc + c, 0)
    lane_map = lambda b, c: (b, 0, c)
    return pl.pallas_call(
        _mlstm_kernel,
        grid=(batch, nc),
        in_specs=[pl.BlockSpec((1, q_t.shape[1], L), lane_map),
                  pl.BlockSpec((L, k.shape[1]), row_map),
                  pl.BlockSpec((1, v_t.shape[1], L), lane_map),
                  pl.BlockSpec((L, v_w), row_map),
                  pl.BlockSpec((L, ML_GATE_PAD), row_map),
                  pl.BlockSpec((2 * ML_HEADS, L), lambda b, c: (0, b * nc + c)),
                  _const_spec((1, v_w))],
        out_specs=pl.BlockSpec((L, v_w), row_map),
        out_shape=jax.ShapeDtypeStruct((batch * seq_len, v_w), BF16),
        scratch_shapes=[pltpu.VMEM((ML_HEADS, ML_VT_ROWS, ML_QK_DIM), F32),
                        pltpu.VMEM((ML_HEADS, 1, 1), F32)],
        compiler_params=_params("parallel", "arbitrary"),
        name="mlstm_chunk",
    )(q_t, k, v_t, og, gcol, grow, head_norm.reshape(1, v_w))


def kernel(x, rel_bias, attn_norm, attn_w_in, attn_lambda_q1, attn_lambda_k1, attn_lambda_q2,
           attn_lambda_k2, attn_subln, attn_w_out, mlstm_norm, mlstm_w_in, mlstm_b_i, mlstm_b_f,
           mlstm_conv_w, mlstm_conv_b, mlstm_head_norm, mlstm_w_out, mlp_norm, mlp_w1, mlp_w2,
           final_norm):
    batch, seq_len, d = x.shape
    t = batch * seq_len
    h = x.reshape(t, d)

    lambda_init = 0.8 - 0.6 * math.exp(-0.3 * 0)
    q_t, k, v_t = _attn_proj(h, attn_norm[0], attn_w_in[0].astype(BF16), batch, seq_len)
    band = _bias_band(rel_bias, ATTN_TILE)
    lam_params = jnp.stack([attn_lambda_q1[0], attn_lambda_k1[0],
                            attn_lambda_q2[0], attn_lambda_k2[0]]).astype(F32)
    attn = _diff_attn(q_t, k, v_t, band, lam_params, attn_subln[0], lambda_init)
    h = _out_mlp(attn.reshape(t, -1), h, attn_w_out[0].astype(BF16), mlp_norm[0],
                 mlp_w1[0].astype(BF16), mlp_w2[0].astype(BF16), final_norm, final_norm=False)

    n_gates = 2 * ML_HEADS
    w_all = jnp.pad(mlstm_w_in[0], ((0, 0), (0, ML_GATE_PAD - n_gates))).astype(BF16)
    gate_b = jnp.pad(jnp.concatenate([mlstm_b_i[0], mlstm_b_f[0]]).astype(F32),
                     (0, ML_GATE_PAD - n_gates))
    q_t, k, v_t, og, gcol, grow = _ml_proj(h, mlstm_norm[0], w_all, mlstm_conv_w[0],
                                           mlstm_conv_b[0], gate_b, batch, seq_len)
    mix = _mlstm(q_t, k, v_t, og, gcol, grow, mlstm_head_norm[0], batch, seq_len)
    out = _out_mlp(mix, h, mlstm_w_out[0].astype(BF16), mlp_norm[1],
                   mlp_w1[1].astype(BF16), mlp_w2[1].astype(BF16), final_norm, final_norm=True)
    return out.reshape(batch, seq_len, d)
```

```python
import functools
import math

import numpy as np
import jax
import jax.numpy as jnp
from jax import lax
from jax.experimental import pallas as pl
from jax.experimental.pallas import tpu as pltpu

F32 = jnp.float32
BF16 = jnp.bfloat16

EPS = 1e-6
NEG = -1e30

DA_HEADS = 8
DA_HEAD_DIM = 64
DA_V_DIM = 2 * DA_HEAD_DIM
REL_BUCKETS = 32
REL_MAX_DIST = 128
DA_VT_ROWS = DA_V_DIM + 16
LOG2E = math.log2(math.e)

ML_HEADS = 4
ML_QK_DIM = 128
ML_V_DIM = 256
CONV_WIDTH = 4
ML_GATE_PAD = 128
ML_VT_ROWS = ML_V_DIM + 16

LANES = 128
BF16_SUBLANES = 16
VMEM_LIMIT = 56 * 1024 * 1024

ROW_TILE = 512
ATTN_TILE = 256
ML_CHUNK = 256
ML_CHUNKS_PER_STEP = 2
ML_CHUNK_AHEAD = 3
ML_PROJ_CHUNK = 256
ML_PROJ_AHEAD = 2
ATTN_PIPE_DEPTH = 4
ATTN_RING = 4
ATTN_TILES_PER_ITER = 2
HALO = BF16_SUBLANES


def _params(*sem):
    return pltpu.CompilerParams(dimension_semantics=sem, vmem_limit_bytes=VMEM_LIMIT)


def _const_spec(shape):
    nd = len(shape)
    return pl.BlockSpec(shape, lambda *_: (0,) * nd)


def _rmsnorm_rows(x, g):
    return x * lax.rsqrt(jnp.mean(x * x, axis=-1, keepdims=True) + EPS) * g


def _attn_proj_kernel(x_ref, g_ref, w_ref, qt_ref, k_ref, vt_ref, *, n_chunk, width, q_scale):
    xn = _rmsnorm_rows(x_ref[...], g_ref[...]).astype(BF16)
    tm = x_ref.shape[0]
    per_part = width // n_chunk
    heads_per_chunk = n_chunk // DA_V_DIM
    ones = jnp.ones((DA_VT_ROWS - DA_V_DIM, tm), BF16)
    for c in range(3 * per_part):
        y = jnp.dot(xn, w_ref[:, c * n_chunk:(c + 1) * n_chunk], preferred_element_type=F32)
        part, pc = divmod(c, per_part)
        if part == 0:
            qt_ref[0, pc * n_chunk:(pc + 1) * n_chunk, :] = jnp.transpose(y * q_scale).astype(BF16)
        elif part == 1:
            k_ref[:, pc * n_chunk:(pc + 1) * n_chunk] = y.astype(BF16)
        else:
            y_t = jnp.transpose(y).astype(BF16)
            for hh in range(heads_per_chunk):
                r0 = (pc * heads_per_chunk + hh) * DA_VT_ROWS
                vt_ref[0, r0:r0 + DA_V_DIM, :] = y_t[hh * DA_V_DIM:(hh + 1) * DA_V_DIM]
                vt_ref[0, r0 + DA_V_DIM:r0 + DA_VT_ROWS, :] = ones


def _attn_proj(x2, g, w_bf16, batch, seq_len):
    t, d = x2.shape
    width = w_bf16.shape[1] // 3
    n_chunk = 512
    tm = ROW_TILE
    assert seq_len % tm == 0 and width % n_chunk == 0 and width == DA_HEADS * DA_V_DIM
    tiles = seq_len // tm
    kern = functools.partial(_attn_proj_kernel, n_chunk=n_chunk, width=width,
                             q_scale=DA_HEAD_DIM ** -0.5 * LOG2E)
    return pl.pallas_call(
        kern,
        grid=(t // tm,),
        in_specs=[pl.BlockSpec((tm, d), lambda i: (i, 0)),
                  _const_spec((1, d)),
                  _const_spec((d, 3 * width))],
        out_specs=(pl.BlockSpec((1, width, tm), lambda i: (i // tiles, 0, i % tiles)),
                   pl.BlockSpec((tm, width), lambda i: (i, 0)),
                   pl.BlockSpec((1, DA_HEADS * DA_VT_ROWS, tm), lambda i: (i // tiles, 0, i % tiles))),
        out_shape=(jax.ShapeDtypeStruct((batch, width, seq_len), BF16),
                   jax.ShapeDtypeStruct((t, width), BF16),
                   jax.ShapeDtypeStruct((batch, DA_HEADS * DA_VT_ROWS, seq_len), BF16)),
        compiler_params=_params("parallel"),
        name="attn_proj",
    )(x2, g.reshape(1, d), w_bf16)


def _bucket_band(tile):
    r = np.arange(3 * tile)[:, None]
    i = np.arange(tile)[None, :]
    dist = i + 2 * tile - r
    max_exact = REL_BUCKETS // 2
    d = np.maximum(dist, 1).astype(np.float32)
    large = max_exact + (np.log(d / np.float32(max_exact)) / np.float32(math.log(REL_MAX_DIST / max_exact))
                         * np.float32(REL_BUCKETS - max_exact)).astype(np.int32)
    large = np.minimum(large, REL_BUCKETS - 1)
    bucket = np.where(dist < max_exact, dist, large)
    bucket = np.where(dist < 0, -1, bucket).astype(np.int32)
    assert (bucket[:tile] == REL_BUCKETS - 1).all()
    return bucket


def _bias_band_kernel(rel_ref, bucket_ref, o_ref):
    h = pl.program_id(0)
    bucket = bucket_ref[...]
    acc = jnp.full(bucket.shape, NEG, F32)
    for b in range(REL_BUCKETS):
        acc = jnp.where(bucket == b, rel_ref[b, h] * LOG2E, acc)
    o_ref[0] = acc


def _bias_band(rel_bias, tile):
    bucket = jnp.asarray(_bucket_band(tile))
    return pl.pallas_call(
        _bias_band_kernel,
        grid=(DA_HEADS,),
        in_specs=[pl.BlockSpec(memory_space=pltpu.SMEM),
                  _const_spec((3 * tile, tile))],
        out_specs=pl.BlockSpec((1, 3 * tile, tile), lambda h: (h, 0, 0)),
        out_shape=jax.ShapeDtypeStruct((DA_HEADS, 3 * tile, tile), F32),
        compiler_params=_params("arbitrary"),
        name="bias_band",
    )(rel_bias, bucket)


def _diff_attn_kernel(lam_ref, qt_ref, k_ref, vt_ref, band_ref, sub_ref, o_ref,
                      qz_ref, m_ref, acc_ref, sb_ref, cm_ref, *, tile, n_tiles, lambda_init):
    head_cols = [slice(h * DA_V_DIM, (h + 1) * DA_V_DIM) for h in range(DA_HEADS)]
    depth = ATTN_PIPE_DEPTH
    slots = sb_ref.shape[0]
    per_iter = ATTN_TILES_PER_ITER

    def tile_span(j):
        return pl.ds(pl.multiple_of(j * tile, tile), tile)

    def scores(j, h):
        return jnp.dot(k_ref[tile_span(j), head_cols[h]], qz_ref[h], preferred_element_type=F32)

    def drain(i, j, h, s_raw):
        kind = jnp.clip(j - (i - 2), 0, 2)
        bias = band_ref[h, pl.ds(pl.multiple_of(kind * tile, tile), tile), :]
        sb = s_raw + jnp.concatenate([bias, bias], axis=1)
        sb_ref[h % slots] = sb
        cm_ref[h % slots] = jnp.max(sb.reshape(tile // 8, 8, 2 * tile), axis=0)

    def consume(j, h):
        m = m_ref[h]
        m_new = jnp.maximum(m, jnp.max(cm_ref[h % slots], axis=0, keepdims=True))
        m_ref[h] = m_new
        p = jnp.exp2((sb_ref[h % slots] - m_new).astype(BF16))
        vt_blk = vt_ref[0, h * DA_VT_ROWS:(h + 1) * DA_VT_ROWS, tile_span(j)]
        return jnp.dot(vt_blk, p, preferred_element_type=F32), jnp.exp2(m - m_new)

    def rescale_add(h, pv, alpha):
        acc_ref[h] = alpha * acc_ref[h] + pv

    def item(j, n):
        return jnp.minimum(j + n // DA_HEADS, n_tiles - 1), n % DA_HEADS

    def run_tiles(i, j0, tiles):
        issued = None
        pending = None
        for n in range(tiles * DA_HEADS):
            jn, hn = item(j0, n + depth)
            nxt = (i, jn, hn, scores(jn, hn))
            pv, alpha = consume(j0 + n // DA_HEADS, n % DA_HEADS)
            if issued is not None:
                drain(*issued)
            if pending is not None:
                rescale_add(*pending)
            issued, pending = nxt, (n % DA_HEADS, pv, alpha)
        drain(*issued)
        rescale_add(*pending)

    def open_query_tile(i):
        sub = lax.broadcasted_iota(jnp.int32, (DA_V_DIM, tile), 0)
        first_map = (sub < DA_HEAD_DIM).astype(F32)
        for h in range(DA_HEADS):
            q_t = qt_ref[0, head_cols[h], tile_span(i)].astype(F32)
            qz_ref[h] = jnp.concatenate([q_t * first_map, q_t * (1.0 - first_map)], axis=1).astype(BF16)
        for n in range(depth):
            drain(i, 0, n, scores(0, n))

    lp = lam_ref[...]
    lam = (jnp.exp(jnp.sum(lp[0:1] * lp[1:2], axis=-1, keepdims=True))
           - jnp.exp(jnp.sum(lp[2:3] * lp[3:4], axis=-1, keepdims=True)) + lambda_init)
    sub_w = sub_ref[...] * (1.0 - lambda_init)

    def query_tile(i, carry):
        m_ref[...] = jnp.full(m_ref.shape, NEG, F32)
        acc_ref[...] = jnp.zeros(acc_ref.shape, F32)

        n_key_tiles = i + 1

        def multi_tile_body(jj, c):
            run_tiles(i, jj * per_iter, per_iter)
            return c

        lax.fori_loop(0, n_key_tiles // per_iter, multi_tile_body, 0)
        for r in range(1, per_iter):
            @pl.when(n_key_tiles % per_iter == r)
            def _(r=r):
                run_tiles(i, n_key_tiles - r, r)

        for h in range(DA_HEADS):
            inv_l = 1.0 / acc_ref[h, DA_V_DIM:DA_V_DIM + 1, :]
            o = (acc_ref[h, 0:DA_V_DIM, :tile] * inv_l[:, :tile]
                 - acc_ref[h, 0:DA_V_DIM, tile:] * (lam * inv_l[:, tile:]))
            o = o * lax.rsqrt(jnp.mean(o * o, axis=0, keepdims=True) + EPS)
            o_ref[0, tile_span(i), head_cols[h]] = (jnp.transpose(o) * sub_w).astype(o_ref.dtype)
        open_query_tile(jnp.minimum(i + 1, n_tiles - 1))
        return carry

    open_query_tile(0)
    lax.fori_loop(0, n_tiles, query_tile, 0)


def _diff_attn(q_t, k, v_t, band, lam_params, subln, lambda_init):
    b, width, s = q_t.shape
    tile = ATTN_TILE
    assert s % tile == 0 and width == DA_HEADS * DA_V_DIM
    kern = functools.partial(_diff_attn_kernel, tile=tile, n_tiles=s // tile, lambda_init=lambda_init)
    return pl.pallas_call(
        kern,
        grid=(b,),
        in_specs=[_const_spec(lam_params.shape),
                  pl.BlockSpec((1, width, s), lambda bi: (bi, 0, 0)),
                  pl.BlockSpec((s, width), lambda bi: (bi, 0)),
                  pl.BlockSpec((1, DA_HEADS * DA_VT_ROWS, s), lambda bi: (bi, 0, 0)),
                  pl.BlockSpec(band.shape, lambda bi: (0, 0, 0), pipeline_mode=pl.Buffered(1)),
                  _const_spec((1, DA_V_DIM))],
        out_specs=pl.BlockSpec((1, s, width), lambda bi: (bi, 0, 0)),
        out_shape=jax.ShapeDtypeStruct((b, s, width), BF16),
        scratch_shapes=[pltpu.VMEM((DA_HEADS, DA_V_DIM, 2 * tile), BF16),
                        pltpu.VMEM((DA_HEADS, 1, 2 * tile), F32),
                        pltpu.VMEM((DA_HEADS, DA_VT_ROWS, 2 * tile), F32),
                        pltpu.VMEM((ATTN_RING, tile, 2 * tile), F32),
                        pltpu.VMEM((ATTN_RING, 8, 2 * tile), F32)],
        compiler_params=_params("parallel"),
        name="diff_attn",
    )(lam_params, q_t, k, v_t, band, subln.reshape(1, DA_V_DIM))


def _out_mlp_kernel(a_ref, h_ref, wo_ref, g_ref, w1_ref, w2_ref, gf_ref, o_ref, u_ref,
                    *, ff_chunk, final_norm):
    h1 = h_ref[...] + jnp.dot(a_ref[...], wo_ref[...], preferred_element_type=F32)
    xn = _rmsnorm_rows(h1, g_ref[...]).astype(BF16)
    d_ff = w1_ref.shape[1]
    for c in range(d_ff // ff_chunk):
        sl = slice(c * ff_chunk, (c + 1) * ff_chunk)
        u = jnp.maximum(jnp.dot(xn, w1_ref[:, sl], preferred_element_type=F32), 0.0)
        u_ref[:, sl] = (u * u).astype(BF16)
    h2 = h1 + jnp.dot(u_ref[...], w2_ref[...], preferred_element_type=F32)
    if final_norm:
        h2 = _rmsnorm_rows(h2, gf_ref[...])
    o_ref[...] = h2


def _out_mlp(a2, h2, wo, g, w1, w2, gf, final_norm):
    t, d = h2.shape
    d_ff = w1.shape[1]
    kern = functools.partial(_out_mlp_kernel, ff_chunk=1024, final_norm=final_norm)
    single = pl.Buffered(1)
    return pl.pallas_call(
        kern,
        grid=(t // ROW_TILE,),
        in_specs=[pl.BlockSpec((ROW_TILE, a2.shape[1]), lambda i: (i, 0)),
                  pl.BlockSpec((ROW_TILE, d), lambda i: (i, 0)),
                  pl.BlockSpec(wo.shape, lambda i: (0, 0), pipeline_mode=single),
                  _const_spec((1, d)),
                  pl.BlockSpec(w1.shape, lambda i: (0, 0), pipeline_mode=single),
                  pl.BlockSpec(w2.shape, lambda i: (0, 0), pipeline_mode=single),
                  _const_spec((1, d))],
        out_specs=pl.BlockSpec((ROW_TILE, d), lambda i: (i, 0)),
        out_shape=jax.ShapeDtypeStruct((t, d), F32),
        scratch_shapes=[pltpu.VMEM((ROW_TILE, d_ff), BF16)],
        compiler_params=_params("parallel"),
        name="out_mlp_final" if final_norm else "out_mlp",
    )(a2, h2, wo, g.reshape(1, d), w1, w2, gf.reshape(1, d))


def _log_sigmoid(x):
    return -(jnp.maximum(-x, 0.0) + jnp.log(1.0 + jnp.exp(-jnp.abs(x))))


def _ml_proj_kernel(x_ref, xh_ref, g_ref, w_ref, cw_ref, cb_ref, gb_ref,
                    qt_ref, k_ref, vt_ref, og_ref, gcol_ref, grow_ref, cs_ref,
                    *, tiles_per_seq, q_scale, qk_w, v_w):
    i = pl.program_id(0)
    tm = x_ref.shape[0]
    nc = ML_PROJ_CHUNK
    g = g_ref[...]
    xn = _rmsnorm_rows(x_ref[...], g).astype(BF16)
    xh = _rmsnorm_rows(xh_ref[...], g).astype(BF16)
    seq_start = (i % tiles_per_seq) == 0
    ones = jnp.ones((ML_VT_ROWS - ML_V_DIM, tm), BF16)
    n_chunks = (qk_w + 2 * v_w) // nc

    def matmul(c):
        if c == n_chunks:
            return jnp.dot(xn, w_ref[:, c * nc:c * nc + ML_GATE_PAD], preferred_element_type=F32)
        return jnp.dot(xn, w_ref[:, c * nc:(c + 1) * nc], preferred_element_type=F32)

    def epilogue(c, y):
        col0 = c * nc
        cols = slice(col0, col0 + nc)
        if col0 < qk_w:
            slot = c % 2
            halo = jnp.dot(xh, w_ref[:, cols], preferred_element_type=F32)
            cs_ref[slot, 0:HALO, :] = jnp.where(seq_start, 0.0, halo)
            cs_ref[slot, HALO:HALO + tm, :] = y
            conv = cb_ref[:, cols]
            for j in range(CONV_WIDTH):
                off = HALO - (CONV_WIDTH - 1) + j
                conv = conv + cw_ref[j:j + 1, cols] * cs_ref[slot, off:off + tm, :]
            act = conv * jax.nn.sigmoid(conv)
            if col0 < qk_w // 2:
                qt_ref[0, cols, :] = jnp.transpose(act * q_scale).astype(BF16)
            else:
                k_ref[:, col0 - qk_w // 2:col0 - qk_w // 2 + nc] = act.astype(BF16)
        elif col0 < qk_w + v_w:
            r0 = ((col0 - qk_w) // ML_V_DIM) * ML_VT_ROWS
            vt_ref[0, r0:r0 + ML_V_DIM, :] = jnp.transpose(y).astype(BF16)
            vt_ref[0, r0 + ML_V_DIM:r0 + ML_VT_ROWS, :] = ones
        else:
            o0 = col0 - qk_w - v_w
            og_ref[:, o0:o0 + nc] = jax.nn.sigmoid(y).astype(BF16)

    qk_chunks = list(range(qk_w // nc))
    light = list(range(qk_w // nc, n_chunks))
    order = []
    while qk_chunks or light:
        order += light[:1] + qk_chunks[:1] + light[1:2]
        qk_chunks, light = qk_chunks[1:], light[2:]
    order.append(n_chunks)
    results = [matmul(c) for c in order[:ML_PROJ_AHEAD]]
    for pos, c in enumerate(order[:-1]):
        if pos + ML_PROJ_AHEAD < len(order):
            results.append(matmul(order[pos + ML_PROJ_AHEAD]))
        epilogue(c, results[pos])
    y = results[-1]
    gates = y + gb_ref[...]
    lane = lax.broadcasted_iota(jnp.int32, gates.shape, 1)
    gates = jnp.where(lane < ML_HEADS, gates, _log_sigmoid(gates))
    gcol_ref[...] = gates
    grow_ref[...] = jnp.transpose(gates)[0:grow_ref.shape[0], :]


def _ml_proj(x2, xnorm_g, w_all, conv_w, conv_b, gate_b, batch, seq_len):
    t, d = x2.shape
    tm = ROW_TILE
    qk_w = 2 * ML_HEADS * ML_QK_DIM
    v_w = ML_HEADS * ML_V_DIM
    assert seq_len % tm == 0 and tm % HALO == 0 and ML_PROJ_CHUNK == ML_V_DIM
    assert w_all.shape[1] == qk_w + 2 * v_w + ML_GATE_PAD
    tiles = seq_len // tm
    kern = functools.partial(_ml_proj_kernel, tiles_per_seq=tiles, q_scale=ML_QK_DIM ** -0.5,
                             qk_w=qk_w, v_w=v_w)
    halo_blocks = tm // HALO
    out_shapes = (jax.ShapeDtypeStruct((batch, qk_w // 2, seq_len), BF16),
                  jax.ShapeDtypeStruct((t, qk_w // 2), BF16),
                  jax.ShapeDtypeStruct((batch, ML_HEADS * ML_VT_ROWS, seq_len), BF16),
                  jax.ShapeDtypeStruct((t, v_w), BF16),
                  jax.ShapeDtypeStruct((t, ML_GATE_PAD), F32),
                  jax.ShapeDtypeStruct((2 * ML_HEADS, t), F32))
    return pl.pallas_call(
        kern,
        grid=(t // tm,),
        in_specs=[pl.BlockSpec((tm, d), lambda i: (i, 0)),
                  pl.BlockSpec((HALO, d), lambda i: (jnp.maximum(i * halo_blocks - 1, 0), 0)),
                  _const_spec((1, d)),
                  pl.BlockSpec(w_all.shape, lambda i: (0, 0), pipeline_mode=pl.Buffered(1)),
                  _const_spec(conv_w.shape),
                  _const_spec((1, qk_w)),
                  _const_spec((1, ML_GATE_PAD))],
        out_specs=(pl.BlockSpec((1, qk_w // 2, tm), lambda i: (i // tiles, 0, i % tiles)),
                   pl.BlockSpec((tm, qk_w // 2), lambda i: (i, 0)),
                   pl.BlockSpec((1, ML_HEADS * ML_VT_ROWS, tm), lambda i: (i // tiles, 0, i % tiles)),
                   pl.BlockSpec((tm, v_w), lambda i: (i, 0)),
                   pl.BlockSpec((tm, ML_GATE_PAD), lambda i: (i, 0)),
                   pl.BlockSpec((2 * ML_HEADS, tm), lambda i: (0, i))),
        out_shape=out_shapes,
        scratch_shapes=[pltpu.VMEM((2, HALO + tm, ML_PROJ_CHUNK), F32)],
        compiler_params=_params("parallel"),
        name="mlstm_proj",
    )(x2, x2, xnorm_g.reshape(1, d), w_all, conv_w, conv_b.reshape(1, qk_w),
      gate_b.reshape(1, ML_GATE_PAD))


def _mlstm_kernel(qt_ref, k_ref, vt_ref, og_ref, gcol_ref, grow_ref, hn_ref, o_ref, s_ref, m_ref):
    @pl.when(pl.program_id(1) == 0)
    def _():
        s_ref[...] = jnp.zeros_like(s_ref)
        m_ref[...] = jnp.zeros_like(m_ref)

    L = ML_CHUNK
    row = lax.broadcasted_iota(jnp.int32, (L, L), 0)
    col = lax.broadcasted_iota(jnp.int32, (L, L), 1)
    key_le_query = row <= col

    def qk_blocks(c, h):
        pos = slice(c * L, (c + 1) * L)
        return (qt_ref[0, h * ML_QK_DIM:(h + 1) * ML_QK_DIM, pos],
                k_ref[pos, h * ML_QK_DIM:(h + 1) * ML_QK_DIM])

    def matmuls(c, h):
        q_t, kh = qk_blocks(c, h)
        s_t = jnp.dot(kh, q_t, preferred_element_type=F32)
        sq = jnp.dot(s_ref[h].astype(BF16), q_t, preferred_element_type=F32)
        return s_t, sq

    def head(c, h, s_t, sq):
        pos = slice(c * L, (c + 1) * L)
        _, kh = qk_blocks(c, h)
        vcols = slice(h * ML_V_DIM, (h + 1) * ML_V_DIM)
        v_t = vt_ref[0, h * ML_VT_ROWS:(h + 1) * ML_VT_ROWS, pos]
        ig_r = grow_ref[h:h + 1, pos]
        lf_r = grow_ref[ML_HEADS + h:ML_HEADS + h + 1, pos]
        ig_c = gcol_ref[pos, h:h + 1]
        lf_c = gcol_ref[pos, ML_HEADS + h:ML_HEADS + h + 1]
        m_prev = m_ref[h]

        b_r = jnp.sum(jnp.where(key_le_query, lf_c, 0.0), axis=0, keepdims=True)
        b_c = jnp.sum(jnp.where(key_le_query, 0.0, lf_r), axis=1, keepdims=True) + lf_c
        b_last = jnp.sum(lf_r, axis=1, keepdims=True)

        d_t = jnp.where(key_le_query, b_r - (b_c - ig_c), NEG)
        m_inter = b_r + m_prev
        m_j = jnp.maximum(m_inter, jnp.max(d_t, axis=0, keepdims=True))
        sc_t = s_t * jnp.exp(d_t - m_j)
        inter = jnp.exp(m_inter - m_j)
        tot = jnp.dot(v_t, sc_t.astype(BF16), preferred_element_type=F32) + inter * sq
        den = tot[ML_V_DIM:ML_V_DIM + 1, :]
        h_t = tot[0:ML_V_DIM, :] * (1.0 / jnp.maximum(jnp.abs(den), jnp.exp(-m_j)))
        y_t = h_t * lax.rsqrt(jnp.mean(h_t * h_t, axis=0, keepdims=True) + EPS)
        y = jnp.transpose(y_t) * hn_ref[:, vcols]
        o_ref[pos, vcols] = (og_ref[pos, vcols].astype(F32) * y).astype(o_ref.dtype)

        a_r = b_last - b_r + ig_r
        m_new = jnp.maximum(b_last + m_prev, jnp.max(a_r, axis=1, keepdims=True))
        w_r = jnp.exp(a_r - m_new)
        decay = jnp.exp(b_last + m_prev - m_new)
        wv_t = (v_t.astype(F32) * w_r).astype(BF16)
        s_ref[h] = decay * s_ref[h] + jnp.dot(wv_t, kh, preferred_element_type=F32)
        m_ref[h] = m_new

    assert ML_CHUNK_AHEAD < ML_HEADS
    items = [(c, h) for c in range(ML_CHUNKS_PER_STEP) for h in range(ML_HEADS)]
    ready = [matmuls(*it) for it in items[:ML_CHUNK_AHEAD]]
    for n, it in enumerate(items):
        if n + ML_CHUNK_AHEAD < len(items):
            ready.append(matmuls(*items[n + ML_CHUNK_AHEAD]))
        head(*it, *ready[n])


def _mlstm(q_t, k, v_t, og, gcol, grow, head_norm, batch, seq_len):
    rows = ML_CHUNK * ML_CHUNKS_PER_STEP
    assert seq_len % rows == 0
    nc = seq_len // rows
    v_w = og.shape[1]
    row_map = lambda b, c: (b * nc + c, 0)
    lane_map = lambda b, c: (b, 0, c)
    return pl.pallas_call(
        _mlstm_kernel,
        grid=(batch, nc),
        in_specs=[pl.BlockSpec((1, q_t.shape[1], rows), lane_map),
                  pl.BlockSpec((rows, k.shape[1]), row_map),
                  pl.BlockSpec((1, v_t.shape[1], rows), lane_map),
                  pl.BlockSpec((rows, v_w), row_map),
                  pl.BlockSpec((rows, ML_GATE_PAD), row_map),
                  pl.BlockSpec((2 * ML_HEADS, rows), lambda b, c: (0, b * nc + c)),
                  _const_spec((1, v_w))],
        out_specs=pl.BlockSpec((rows, v_w), row_map),
        out_shape=jax.ShapeDtypeStruct((batch * seq_len, v_w), BF16),
        scratch_shapes=[pltpu.VMEM((ML_HEADS, ML_VT_ROWS, ML_QK_DIM), F32),
                        pltpu.VMEM((ML_HEADS, 1, 1), F32)],
        compiler_params=_params("parallel", "arbitrary"),
        name="mlstm_chunk",
    )(q_t, k, v_t, og, gcol, grow, head_norm.reshape(1, v_w))


def kernel(x, rel_bias, attn_norm, attn_w_in, attn_lambda_q1, attn_lambda_k1, attn_lambda_q2,
           attn_lambda_k2, attn_subln, attn_w_out, mlstm_norm, mlstm_w_in, mlstm_b_i, mlstm_b_f,
           mlstm_conv_w, mlstm_conv_b, mlstm_head_norm, mlstm_w_out, mlp_norm, mlp_w1, mlp_w2,
           final_norm):
    batch, seq_len, d = x.shape
    t = batch * seq_len
    h = x.reshape(t, d)

    lambda_init = 0.8 - 0.6 * math.exp(-0.3 * 0)
    q_t, k, v_t = _attn_proj(h, attn_norm[0], attn_w_in[0].astype(BF16), batch, seq_len)
    band = _bias_band(rel_bias, ATTN_TILE)
    lam_params = jnp.stack([attn_lambda_q1[0], attn_lambda_k1[0],
                            attn_lambda_q2[0], attn_lambda_k2[0]]).astype(F32)
    attn = _diff_attn(q_t, k, v_t, band, lam_params, attn_subln[0], lambda_init)
    h = _out_mlp(attn.reshape(t, -1), h, attn_w_out[0].astype(BF16), mlp_norm[0],
                 mlp_w1[0].astype(BF16), mlp_w2[0].astype(BF16), final_norm, final_norm=False)

    n_gates = 2 * ML_HEADS
    w_all = jnp.pad(mlstm_w_in[0], ((0, 0), (0, ML_GATE_PAD - n_gates))).astype(BF16)
    gate_b = jnp.pad(jnp.concatenate([mlstm_b_i[0], mlstm_b_f[0]]).astype(F32),
                     (0, ML_GATE_PAD - n_gates))
    q_t, k, v_t, og, gcol, grow = _ml_proj(h, mlstm_norm[0], w_all, mlstm_conv_w[0],
                                           mlstm_conv_b[0], gate_b, batch, seq_len)
    mix = _mlstm(q_t, k, v_t, og, gcol, grow, mlstm_head_norm[0], batch, seq_len)
    out = _out_mlp(mix, h, mlstm_w_out[0].astype(BF16), mlp_norm[1],
                   mlp_w1[1].astype(BF16), mlp_w2[1].astype(BF16), final_norm, final_norm=True)
    return out.reshape(batch, seq_len, d)
```

```python
import functools
import math

import numpy as np
import jax
import jax.numpy as jnp
from jax import lax
from jax.experimental import pallas as pl
from jax.experimental.pallas import tpu as pltpu

F32 = jnp.float32
BF16 = jnp.bfloat16

EPS = 1e-6
NEG = -1e30

DA_HEADS = 8
DA_HEAD_DIM = 64
DA_V_DIM = 2 * DA_HEAD_DIM
REL_BUCKETS = 32
REL_MAX_DIST = 128
DA_VT_ROWS = DA_V_DIM + 16
LOG2E = math.log2(math.e)

ML_HEADS = 4
ML_QK_DIM = 128
ML_V_DIM = 256
CONV_WIDTH = 4
ML_GATE_PAD = 128
ML_VT_ROWS = ML_V_DIM + 16

LANES = 128
BF16_SUBLANES = 16
VMEM_LIMIT = 56 * 1024 * 1024

ROW_TILE = 512
ATTN_TILE = 256
ML_CHUNK = 256
ML_PROJ_CHUNK = 256
ML_PROJ_AHEAD = 2
ATTN_PIPE_DEPTH = 4
ATTN_RING = 4
ATTN_TILES_PER_ITER = 2
HALO = BF16_SUBLANES


def _params(*sem):
    return pltpu.CompilerParams(dimension_semantics=sem, vmem_limit_bytes=VMEM_LIMIT)


def _const_spec(shape):
    nd = len(shape)
    return pl.BlockSpec(shape, lambda *_: (0,) * nd)


def _rmsnorm_rows(x, g):
    return x * lax.rsqrt(jnp.mean(x * x, axis=-1, keepdims=True) + EPS) * g


def _attn_proj_kernel(x_ref, g_ref, w_ref, qt_ref, k_ref, vt_ref, *, n_chunk, width, q_scale):
    xn = _rmsnorm_rows(x_ref[...], g_ref[...]).astype(BF16)
    tm = x_ref.shape[0]
    per_part = width // n_chunk
    heads_per_chunk = n_chunk // DA_V_DIM
    ones = jnp.ones((DA_VT_ROWS - DA_V_DIM, tm), BF16)
    for c in range(3 * per_part):
        y = jnp.dot(xn, w_ref[:, c * n_chunk:(c + 1) * n_chunk], preferred_element_type=F32)
        part, pc = divmod(c, per_part)
        if part == 0:
            qt_ref[0, pc * n_chunk:(pc + 1) * n_chunk, :] = jnp.transpose(y * q_scale).astype(BF16)
        elif part == 1:
            k_ref[:, pc * n_chunk:(pc + 1) * n_chunk] = y.astype(BF16)
        else:
            y_t = jnp.transpose(y).astype(BF16)
            for hh in range(heads_per_chunk):
                r0 = (pc * heads_per_chunk + hh) * DA_VT_ROWS
                vt_ref[0, r0:r0 + DA_V_DIM, :] = y_t[hh * DA_V_DIM:(hh + 1) * DA_V_DIM]
                vt_ref[0, r0 + DA_V_DIM:r0 + DA_VT_ROWS, :] = ones


def _attn_proj(x2, g, w_bf16, batch, seq_len):
    t, d = x2.shape
    width = w_bf16.shape[1] // 3
    n_chunk = 512
    tm = ROW_TILE
    assert seq_len % tm == 0 and width % n_chunk == 0 and width == DA_HEADS * DA_V_DIM
    tiles = seq_len // tm
    kern = functools.partial(_attn_proj_kernel, n_chunk=n_chunk, width=width,
                             q_scale=DA_HEAD_DIM ** -0.5 * LOG2E)
    return pl.pallas_call(
        kern,
        grid=(t // tm,),
        in_specs=[pl.BlockSpec((tm, d), lambda i: (i, 0)),
                  _const_spec((1, d)),
                  _const_spec((d, 3 * width))],
        out_specs=(pl.BlockSpec((1, width, tm), lambda i: (i // tiles, 0, i % tiles)),
                   pl.BlockSpec((tm, width), lambda i: (i, 0)),
                   pl.BlockSpec((1, DA_HEADS * DA_VT_ROWS, tm), lambda i: (i // tiles, 0, i % tiles))),
        out_shape=(jax.ShapeDtypeStruct((batch, width, seq_len), BF16),
                   jax.ShapeDtypeStruct((t, width), BF16),
                   jax.ShapeDtypeStruct((batch, DA_HEADS * DA_VT_ROWS, seq_len), BF16)),
        compiler_params=_params("parallel"),
        name="attn_proj",
    )(x2, g.reshape(1, d), w_bf16)


def _bucket_band(tile):
    r = np.arange(3 * tile)[:, None]
    i = np.arange(tile)[None, :]
    dist = i + 2 * tile - r
    max_exact = REL_BUCKETS // 2
    d = np.maximum(dist, 1).astype(np.float32)
    large = max_exact + (np.log(d / np.float32(max_exact)) / np.float32(math.log(REL_MAX_DIST / max_exact))
                         * np.float32(REL_BUCKETS - max_exact)).astype(np.int32)
    large = np.minimum(large, REL_BUCKETS - 1)
    bucket = np.where(dist < max_exact, dist, large)
    bucket = np.where(dist < 0, -1, bucket).astype(np.int32)
    assert (bucket[:tile] == REL_BUCKETS - 1).all()
    return bucket


def _bias_band_kernel(rel_ref, bucket_ref, o_ref):
    h = pl.program_id(0)
    bucket = bucket_ref[...]
    acc = jnp.full(bucket.shape, NEG, F32)
    for b in range(REL_BUCKETS):
        acc = jnp.where(bucket == b, rel_ref[b, h] * LOG2E, acc)
    o_ref[0] = acc


def _bias_band(rel_bias, tile):
    bucket = jnp.asarray(_bucket_band(tile))
    return pl.pallas_call(
        _bias_band_kernel,
        grid=(DA_HEADS,),
        in_specs=[pl.BlockSpec(memory_space=pltpu.SMEM),
                  _const_spec((3 * tile, tile))],
        out_specs=pl.BlockSpec((1, 3 * tile, tile), lambda h: (h, 0, 0)),
        out_shape=jax.ShapeDtypeStruct((DA_HEADS, 3 * tile, tile), F32),
        compiler_params=_params("arbitrary"),
        name="bias_band",
    )(rel_bias, bucket)


def _diff_attn_kernel(lam_ref, qt_ref, k_ref, vt_ref, band_ref, sub_ref, o_ref,
                      qz_ref, m_ref, acc_ref, sb_ref, cm_ref, *, tile, n_tiles, lambda_init):
    head_cols = [slice(h * DA_V_DIM, (h + 1) * DA_V_DIM) for h in range(DA_HEADS)]
    depth = ATTN_PIPE_DEPTH
    slots = sb_ref.shape[0]
    per_iter = ATTN_TILES_PER_ITER

    def tile_span(j):
        return pl.ds(pl.multiple_of(j * tile, tile), tile)

    def scores(j, h):
        return jnp.dot(k_ref[tile_span(j), head_cols[h]], qz_ref[h], preferred_element_type=F32)

    def drain(i, j, h, s_raw):
        kind = jnp.clip(j - (i - 2), 0, 2)
        bias = band_ref[h, pl.ds(pl.multiple_of(kind * tile, tile), tile), :]
        sb = s_raw + jnp.concatenate([bias, bias], axis=1)
        sb_ref[h % slots] = sb
        cm_ref[h % slots] = jnp.max(sb.reshape(tile // 8, 8, 2 * tile), axis=0)

    def consume(j, h):
        m = m_ref[h]
        m_new = jnp.maximum(m, jnp.max(cm_ref[h % slots], axis=0, keepdims=True))
        m_ref[h] = m_new
        p = jnp.exp2((sb_ref[h % slots] - m_new).astype(BF16))
        vt_blk = vt_ref[0, h * DA_VT_ROWS:(h + 1) * DA_VT_ROWS, tile_span(j)]
        return jnp.dot(vt_blk, p, preferred_element_type=F32), jnp.exp2(m - m_new)

    def rescale_add(h, pv, alpha):
        acc_ref[h] = alpha * acc_ref[h] + pv

    def item(j, n):
        return jnp.minimum(j + n // DA_HEADS, n_tiles - 1), n % DA_HEADS

    def run_tiles(i, j0, tiles):
        issued = None
        pending = None
        for n in range(tiles * DA_HEADS):
            jn, hn = item(j0, n + depth)
            nxt = (i, jn, hn, scores(jn, hn))
            pv, alpha = consume(j0 + n // DA_HEADS, n % DA_HEADS)
            if issued is not None:
                drain(*issued)
            if pending is not None:
                rescale_add(*pending)
            issued, pending = nxt, (n % DA_HEADS, pv, alpha)
        drain(*issued)
        rescale_add(*pending)

    def open_query_tile(i):
        sub = lax.broadcasted_iota(jnp.int32, (DA_V_DIM, tile), 0)
        first_map = (sub < DA_HEAD_DIM).astype(F32)
        for h in range(DA_HEADS):
            q_t = qt_ref[0, head_cols[h], tile_span(i)].astype(F32)
            qz_ref[h] = jnp.concatenate([q_t * first_map, q_t * (1.0 - first_map)], axis=1).astype(BF16)
        for n in range(depth):
            drain(i, 0, n, scores(0, n))

    lp = lam_ref[...]
    lam = (jnp.exp(jnp.sum(lp[0:1] * lp[1:2], axis=-1, keepdims=True))
           - jnp.exp(jnp.sum(lp[2:3] * lp[3:4], axis=-1, keepdims=True)) + lambda_init)
    sub_w = sub_ref[...] * (1.0 - lambda_init)

    def query_tile(i, carry):
        m_ref[...] = jnp.full(m_ref.shape, NEG, F32)
        acc_ref[...] = jnp.zeros(acc_ref.shape, F32)

        n_key_tiles = i + 1

        def multi_tile_body(jj, c):
            run_tiles(i, jj * per_iter, per_iter)
            return c

        lax.fori_loop(0, n_key_tiles // per_iter, multi_tile_body, 0)
        for r in range(1, per_iter):
            @pl.when(n_key_tiles % per_iter == r)
            def _(r=r):
                run_tiles(i, n_key_tiles - r, r)

        for h in range(DA_HEADS):
            inv_l = 1.0 / acc_ref[h, DA_V_DIM:DA_V_DIM + 1, :]
            o = (acc_ref[h, 0:DA_V_DIM, :tile] * inv_l[:, :tile]
                 - acc_ref[h, 0:DA_V_DIM, tile:] * (lam * inv_l[:, tile:]))
            o = o * lax.rsqrt(jnp.mean(o * o, axis=0, keepdims=True) + EPS)
            o_ref[0, tile_span(i), head_cols[h]] = (jnp.transpose(o) * sub_w).astype(o_ref.dtype)
        open_query_tile(jnp.minimum(i + 1, n_tiles - 1))
        return carry

    open_query_tile(0)
    lax.fori_loop(0, n_tiles, query_tile, 0)


def _diff_attn(q_t, k, v_t, band, lam_params, subln, lambda_init):
    b, width, s = q_t.shape
    tile = ATTN_TILE
    assert s % tile == 0 and width == DA_HEADS * DA_V_DIM
    kern = functools.partial(_diff_attn_kernel, tile=tile, n_tiles=s // tile, lambda_init=lambda_init)
    return pl.pallas_call(
        kern,
        grid=(b,),
        in_specs=[_const_spec(lam_params.shape),
                  pl.BlockSpec((1, width, s), lambda bi: (bi, 0, 0)),
                  pl.BlockSpec((s, width), lambda bi: (bi, 0)),
                  pl.BlockSpec((1, DA_HEADS * DA_VT_ROWS, s), lambda bi: (bi, 0, 0)),
                  pl.BlockSpec(band.shape, lambda bi: (0, 0, 0), pipeline_mode=pl.Buffered(1)),
                  _const_spec((1, DA_V_DIM))],
        out_specs=pl.BlockSpec((1, s, width), lambda bi: (bi, 0, 0)),
        out_shape=jax.ShapeDtypeStruct((b, s, width), BF16),
        scratch_shapes=[pltpu.VMEM((DA_HEADS, DA_V_DIM, 2 * tile), BF16),
                        pltpu.VMEM((DA_HEADS, 1, 2 * tile), F32),
                        pltpu.VMEM((DA_HEADS, DA_VT_ROWS, 2 * tile), F32),
                        pltpu.VMEM((ATTN_RING, tile, 2 * tile), F32),
                        pltpu.VMEM((ATTN_RING, 8, 2 * tile), F32)],
        compiler_params=_params("parallel"),
        name="diff_attn",
    )(lam_params, q_t, k, v_t, band, subln.reshape(1, DA_V_DIM))


def _out_mlp_kernel(a_ref, h_ref, wo_ref, g_ref, w1_ref, w2_ref, gf_ref, o_ref, u_ref,
                    *, ff_chunk, final_norm):
    h1 = h_ref[...] + jnp.dot(a_ref[...], wo_ref[...], preferred_element_type=F32)
    xn = _rmsnorm_rows(h1, g_ref[...]).astype(BF16)
    d_ff = w1_ref.shape[1]
    for c in range(d_ff // ff_chunk):
        sl = slice(c * ff_chunk, (c + 1) * ff_chunk)
        u = jnp.maximum(jnp.dot(xn, w1_ref[:, sl], preferred_element_type=F32), 0.0)
        u_ref[:, sl] = (u * u).astype(BF16)
    h2 = h1 + jnp.dot(u_ref[...], w2_ref[...], preferred_element_type=F32)
    if final_norm:
        h2 = _rmsnorm_rows(h2, gf_ref[...])
    o_ref[...] = h2


def _out_mlp(a2, h2, wo, g, w1, w2, gf, final_norm):
    t, d = h2.shape
    d_ff = w1.shape[1]
    kern = functools.partial(_out_mlp_kernel, ff_chunk=1024, final_norm=final_norm)
    single = pl.Buffered(1)
    return pl.pallas_call(
        kern,
        grid=(t // ROW_TILE,),
        in_specs=[pl.BlockSpec((ROW_TILE, a2.shape[1]), lambda i: (i, 0)),
                  pl.BlockSpec((ROW_TILE, d), lambda i: (i, 0)),
                  pl.BlockSpec(wo.shape, lambda i: (0, 0), pipeline_mode=single),
                  _const_spec((1, d)),
                  pl.BlockSpec(w1.shape, lambda i: (0, 0), pipeline_mode=single),
                  pl.BlockSpec(w2.shape, lambda i: (0, 0), pipeline_mode=single),
                  _const_spec((1, d))],
        out_specs=pl.BlockSpec((ROW_TILE, d), lambda i: (i, 0)),
        out_shape=jax.ShapeDtypeStruct((t, d), F32),
        scratch_shapes=[pltpu.VMEM((ROW_TILE, d_ff), BF16)],
        compiler_params=_params("parallel"),
        name="out_mlp_final" if final_norm else "out_mlp",
    )(a2, h2, wo, g.reshape(1, d), w1, w2, gf.reshape(1, d))


def _log_sigmoid(x):
    return -(jnp.maximum(-x, 0.0) + jnp.log(1.0 + jnp.exp(-jnp.abs(x))))


def _mlstm_layer_kernel(x_ref, xh_ref, g_ref, w_ref, cw_ref, cb_ref, gb_ref, hn_ref, o_ref,
                        cs_ref, qt_ref, k_ref, vt_ref, og_ref, gcol_ref, grow_ref, s_ref, m_ref,
                        *, tiles_per_seq, q_scale, qk_w, v_w):
    i = pl.program_id(0)
    tm = x_ref.shape[0]
    nc = ML_PROJ_CHUNK
    L = ML_CHUNK
    seq_start = (i % tiles_per_seq) == 0

    @pl.when(seq_start)
    def _():
        s_ref[...] = jnp.zeros_like(s_ref)
        m_ref[...] = jnp.zeros_like(m_ref)

    g = g_ref[...]
    xn = _rmsnorm_rows(x_ref[...], g).astype(BF16)
    xh = _rmsnorm_rows(xh_ref[...], g).astype(BF16)
    ones = jnp.ones((ML_VT_ROWS - ML_V_DIM, tm), BF16)
    n_chunks = (qk_w + 2 * v_w) // nc
    gate_chunk = n_chunks

    def proj_matmul(c):
        if c == gate_chunk:
            return jnp.dot(xn, w_ref[:, c * nc:c * nc + ML_GATE_PAD], preferred_element_type=F32)
        return jnp.dot(xn, w_ref[:, c * nc:(c + 1) * nc], preferred_element_type=F32)

    def proj_epilogue(c, y):
        col0 = c * nc
        cols = slice(col0, col0 + nc)
        if c == gate_chunk:
            gates = y + gb_ref[...]
            lane = lax.broadcasted_iota(jnp.int32, gates.shape, 1)
            gates = jnp.where(lane < ML_HEADS, gates, _log_sigmoid(gates))
            gcol_ref[...] = gates
            grow_ref[...] = jnp.transpose(gates)[0:grow_ref.shape[0], :]
        elif col0 < qk_w:
            slot = c % 2
            halo = jnp.dot(xh, w_ref[:, cols], preferred_element_type=F32)
            cs_ref[slot, 0:HALO, :] = jnp.where(seq_start, 0.0, halo)
            cs_ref[slot, HALO:HALO + tm, :] = y
            conv = cb_ref[:, cols]
            for j in range(CONV_WIDTH):
                off = HALO - (CONV_WIDTH - 1) + j
                conv = conv + cw_ref[j:j + 1, cols] * cs_ref[slot, off:off + tm, :]
            act = conv * jax.nn.sigmoid(conv)
            if col0 < qk_w // 2:
                qt_ref[cols, :] = jnp.transpose(act * q_scale).astype(BF16)
            else:
                k_ref[:, col0 - qk_w // 2:col0 - qk_w // 2 + nc] = act.astype(BF16)
        elif col0 < qk_w + v_w:
            r0 = ((col0 - qk_w) // ML_V_DIM) * ML_VT_ROWS
            vt_ref[r0:r0 + ML_V_DIM, :] = jnp.transpose(y).astype(BF16)
            vt_ref[r0 + ML_V_DIM:r0 + ML_VT_ROWS, :] = ones
        else:
            o0 = col0 - qk_w - v_w
            og_ref[:, o0:o0 + nc] = jax.nn.sigmoid(y).astype(BF16)

    row = lax.broadcasted_iota(jnp.int32, (L, L), 0)
    col = lax.broadcasted_iota(jnp.int32, (L, L), 1)
    key_le_query = row <= col

    def qk_blocks(c, h):
        pos = slice(c * L, (c + 1) * L)
        return (qt_ref[h * ML_QK_DIM:(h + 1) * ML_QK_DIM, pos],
                k_ref[pos, h * ML_QK_DIM:(h + 1) * ML_QK_DIM])

    def rec_matmuls(c, h):
        q_t, kh = qk_blocks(c, h)
        s_t = jnp.dot(kh, q_t, preferred_element_type=F32)
        sq = jnp.dot(s_ref[h].astype(BF16), q_t, preferred_element_type=F32)
        return s_t, sq

    def rec_head(c, h, s_t, sq):
        pos = slice(c * L, (c + 1) * L)
        _, kh = qk_blocks(c, h)
        vcols = slice(h * ML_V_DIM, (h + 1) * ML_V_DIM)
        v_t = vt_ref[h * ML_VT_ROWS:(h + 1) * ML_VT_ROWS, pos]
        ig_r = grow_ref[h:h + 1, pos]
        lf_r = grow_ref[ML_HEADS + h:ML_HEADS + h + 1, pos]
        ig_c = gcol_ref[pos, h:h + 1]
        lf_c = gcol_ref[pos, ML_HEADS + h:ML_HEADS + h + 1]
        m_prev = m_ref[h]

        b_r = jnp.sum(jnp.where(key_le_query, lf_c, 0.0), axis=0, keepdims=True)
        b_c = jnp.sum(jnp.where(key_le_query, 0.0, lf_r), axis=1, keepdims=True) + lf_c
        b_last = jnp.sum(lf_r, axis=1, keepdims=True)

        d_t = jnp.where(key_le_query, b_r - (b_c - ig_c), NEG)
        m_inter = b_r + m_prev
        m_j = jnp.maximum(m_inter, jnp.max(d_t, axis=0, keepdims=True))
        sc_t = s_t * jnp.exp(d_t - m_j)
        inter = jnp.exp(m_inter - m_j)
        tot = jnp.dot(v_t, sc_t.astype(BF16), preferred_element_type=F32) + inter * sq
        den = tot[ML_V_DIM:ML_V_DIM + 1, :]
        h_t = tot[0:ML_V_DIM, :] * (1.0 / jnp.maximum(jnp.abs(den), jnp.exp(-m_j)))
        y_t = h_t * lax.rsqrt(jnp.mean(h_t * h_t, axis=0, keepdims=True) + EPS)
        y = jnp.transpose(y_t) * hn_ref[:, vcols]
        o_ref[pos, vcols] = (og_ref[pos, vcols].astype(F32) * y).astype(o_ref.dtype)

        a_r = b_last - b_r + ig_r
        m_new = jnp.maximum(b_last + m_prev, jnp.max(a_r, axis=1, keepdims=True))
        w_r = jnp.exp(a_r - m_new)
        decay = jnp.exp(b_last + m_prev - m_new)
        wv_t = (v_t.astype(F32) * w_r).astype(BF16)
        s_ref[h] = decay * s_ref[h] + jnp.dot(wv_t, kh, preferred_element_type=F32)
        m_ref[h] = m_new

    qk_chunks = list(range(qk_w // nc))
    v_chunks = list(range(qk_w // nc, (qk_w + v_w) // nc))
    o_chunks = list(range((qk_w + v_w) // nc, n_chunks))
    first = [gate_chunk]
    while qk_chunks or v_chunks:
        first += v_chunks[:1] + qk_chunks[:1]
        qk_chunks, v_chunks = qk_chunks[1:], v_chunks[1:]
    rec_items = [(c, h) for c in range(tm // L) for h in range(ML_HEADS)]
    assert len(o_chunks) == ML_HEADS and ML_PROJ_AHEAD < ML_HEADS
    stages = [("proj", c) for c in first]
    for h in range(ML_HEADS):
        stages += [("proj", o_chunks[h]), ("rec", rec_items[h])]
    stages += [("rec", it) for it in rec_items[ML_HEADS:]]

    def issue(stage):
        kind, arg = stage
        return proj_matmul(arg) if kind == "proj" else rec_matmuls(*arg)

    ahead = ML_PROJ_AHEAD
    n_proj = len(first)
    issued = {}
    for n in range(min(ahead, n_proj)):
        issued[n] = issue(stages[n])
    for n, stage in enumerate(stages):
        nxt = n + ahead
        if nxt < len(stages) and nxt not in issued:
            if stages[nxt][0] == "proj" or n >= n_proj - 1:
                issued[nxt] = issue(stages[nxt])
        if n not in issued:
            issued[n] = issue(stage)
        kind, arg = stage
        if kind == "proj":
            proj_epilogue(arg, issued.pop(n))
        else:
            rec_head(*arg, *issued.pop(n))


def _mlstm_layer(x2, xnorm_g, w_all, conv_w, conv_b, gate_b, head_norm, batch, seq_len):
    t, d = x2.shape
    tm = ROW_TILE
    qk_w = 2 * ML_HEADS * ML_QK_DIM
    v_w = ML_HEADS * ML_V_DIM
    assert seq_len % tm == 0 and tm % HALO == 0 and tm % ML_CHUNK == 0 and ML_PROJ_CHUNK == ML_V_DIM
    assert w_all.shape[1] == qk_w + 2 * v_w + ML_GATE_PAD
    tiles = seq_len // tm
    kern = functools.partial(_mlstm_layer_kernel, tiles_per_seq=tiles, q_scale=ML_QK_DIM ** -0.5,
                             qk_w=qk_w, v_w=v_w)
    halo_blocks = tm // HALO
    return pl.pallas_call(
        kern,
        grid=(t // tm,),
        in_specs=[pl.BlockSpec((tm, d), lambda i: (i, 0)),
                  pl.BlockSpec((HALO, d), lambda i: (jnp.maximum(i * halo_blocks - 1, 0), 0)),
                  _const_spec((1, d)),
                  pl.BlockSpec(w_all.shape, lambda i: (0, 0), pipeline_mode=pl.Buffered(1)),
                  _const_spec(conv_w.shape),
                  _const_spec((1, qk_w)),
                  _const_spec((1, ML_GATE_PAD)),
                  _const_spec((1, v_w))],
        out_specs=pl.BlockSpec((tm, v_w), lambda i: (i, 0)),
        out_shape=jax.ShapeDtypeStruct((t, v_w), BF16),
        scratch_shapes=[pltpu.VMEM((2, HALO + tm, ML_PROJ_CHUNK), F32),
                        pltpu.VMEM((qk_w // 2, tm), BF16),
                        pltpu.VMEM((tm, qk_w // 2), BF16),
                        pltpu.VMEM((ML_HEADS * ML_VT_ROWS, tm), BF16),
                        pltpu.VMEM((tm, v_w), BF16),
                        pltpu.VMEM((tm, ML_GATE_PAD), F32),
                        pltpu.VMEM((2 * ML_HEADS, tm), F32),
                        pltpu.VMEM((ML_HEADS, ML_VT_ROWS, ML_QK_DIM), F32),
                        pltpu.VMEM((ML_HEADS, 1, 1), F32)],
        compiler_params=_params("arbitrary"),
        name="mlstm_layer",
    )(x2, x2, xnorm_g.reshape(1, d), w_all, conv_w, conv_b.reshape(1, qk_w),
      gate_b.reshape(1, ML_GATE_PAD), head_norm.reshape(1, v_w))


def kernel(x, rel_bias, attn_norm, attn_w_in, attn_lambda_q1, attn_lambda_k1, attn_lambda_q2,
           attn_lambda_k2, attn_subln, attn_w_out, mlstm_norm, mlstm_w_in, mlstm_b_i, mlstm_b_f,
           mlstm_conv_w, mlstm_conv_b, mlstm_head_norm, mlstm_w_out, mlp_norm, mlp_w1, mlp_w2,
           final_norm):
    batch, seq_len, d = x.shape
    t = batch * seq_len
    h = x.reshape(t, d)

    lambda_init = 0.8 - 0.6 * math.exp(-0.3 * 0)
    q_t, k, v_t = _attn_proj(h, attn_norm[0], attn_w_in[0].astype(BF16), batch, seq_len)
    band = _bias_band(rel_bias, ATTN_TILE)
    lam_params = jnp.stack([attn_lambda_q1[0], attn_lambda_k1[0],
                            attn_lambda_q2[0], attn_lambda_k2[0]]).astype(F32)
    attn = _diff_attn(q_t, k, v_t, band, lam_params, attn_subln[0], lambda_init)
    h = _out_mlp(attn.reshape(t, -1), h, attn_w_out[0].astype(BF16), mlp_norm[0],
                 mlp_w1[0].astype(BF16), mlp_w2[0].astype(BF16), final_norm, final_norm=False)

    n_gates = 2 * ML_HEADS
    w_all = jnp.pad(mlstm_w_in[0], ((0, 0), (0, ML_GATE_PAD - n_gates))).astype(BF16)
    gate_b = jnp.pad(jnp.concatenate([mlstm_b_i[0], mlstm_b_f[0]]).astype(F32),
                     (0, ML_GATE_PAD - n_gates))
    mix = _mlstm_layer(h, mlstm_norm[0], w_all, mlstm_conv_w[0], mlstm_conv_b[0], gate_b,
                       mlstm_head_norm[0], batch, seq_len)
    out = _out_mlp(mix, h, mlstm_w_out[0].astype(BF16), mlp_norm[1],
                   mlp_w1[1].astype(BF16), mlp_w2[1].astype(BF16), final_norm, final_norm=True)
    return out.reshape(batch, seq_len, d)
```

```python
import functools
import math

import numpy as np
import jax
import jax.numpy as jnp
from jax import lax
from jax.experimental import pallas as pl
from jax.experimental.pallas import tpu as pltpu

F32 = jnp.float32
BF16 = jnp.bfloat16

EPS = 1e-6
NEG = -1e30

LANES = 128
F32_SUBLANES = 8
BF16_SUBLANES = 16
VMEM_LIMIT = 56 * 1024 * 1024

DA_HEADS = 8
DA_HEAD_DIM = 64
DA_V_DIM = 2 * DA_HEAD_DIM
REL_BUCKETS = 32
REL_MAX_DIST = 128
DA_VT_ROWS = DA_V_DIM + BF16_SUBLANES
LOG2E = math.log2(math.e)

ML_HEADS = 4
ML_QK_DIM = 128
ML_V_DIM = 256
CONV_WIDTH = 4
ML_GATE_PAD = LANES
ML_VT_ROWS = ML_V_DIM + BF16_SUBLANES

ROW_TILE = 512
ATTN_PROJ_CHUNK = 512
MLP_FF_CHUNK = 1024
ATTN_TILE = 256
ML_CHUNK = 256
ML_PROJ_CHUNK = 256
ML_PROJ_AHEAD = 2
ATTN_PIPE_DEPTH = 4
ATTN_RING = 4
ATTN_TILES_PER_ITER = 2
HALO = BF16_SUBLANES


def _params(*sem):
    return pltpu.CompilerParams(dimension_semantics=sem, vmem_limit_bytes=VMEM_LIMIT)


def _const_spec(shape):
    nd = len(shape)
    return pl.BlockSpec(shape, lambda *_: (0,) * nd)


def _rmsnorm_rows(x, g):
    return x * lax.rsqrt(jnp.mean(x * x, axis=-1, keepdims=True) + EPS) * g


def _attn_proj_kernel(x_ref, g_ref, w_ref, qt_ref, k_ref, vt_ref, *, n_chunk, width, q_scale):
    xn = _rmsnorm_rows(x_ref[...], g_ref[...]).astype(BF16)
    tm = x_ref.shape[0]
    per_part = width // n_chunk
    heads_per_chunk = n_chunk // DA_V_DIM
    ones = jnp.ones((DA_VT_ROWS - DA_V_DIM, tm), BF16)
    for c in range(3 * per_part):
        y = jnp.dot(xn, w_ref[:, c * n_chunk:(c + 1) * n_chunk], preferred_element_type=F32)
        part, pc = divmod(c, per_part)
        if part == 0:
            qt_ref[0, pc * n_chunk:(pc + 1) * n_chunk, :] = jnp.transpose(y * q_scale).astype(BF16)
        elif part == 1:
            k_ref[:, pc * n_chunk:(pc + 1) * n_chunk] = y.astype(BF16)
        else:
            y_t = jnp.transpose(y).astype(BF16)
            for hh in range(heads_per_chunk):
                r0 = (pc * heads_per_chunk + hh) * DA_VT_ROWS
                vt_ref[0, r0:r0 + DA_V_DIM, :] = y_t[hh * DA_V_DIM:(hh + 1) * DA_V_DIM]
                vt_ref[0, r0 + DA_V_DIM:r0 + DA_VT_ROWS, :] = ones


def _attn_proj(x2, g, w_bf16, batch, seq_len):
    t, d = x2.shape
    width = w_bf16.shape[1] // 3
    n_chunk = ATTN_PROJ_CHUNK
    tm = ROW_TILE
    assert seq_len % tm == 0 and width % n_chunk == 0 and width == DA_HEADS * DA_V_DIM
    tiles = seq_len // tm
    kern = functools.partial(_attn_proj_kernel, n_chunk=n_chunk, width=width,
                             q_scale=DA_HEAD_DIM ** -0.5 * LOG2E)
    return pl.pallas_call(
        kern,
        grid=(t // tm,),
        in_specs=[pl.BlockSpec((tm, d), lambda i: (i, 0)),
                  _const_spec((1, d)),
                  _const_spec((d, 3 * width))],
        out_specs=(pl.BlockSpec((1, width, tm), lambda i: (i // tiles, 0, i % tiles)),
                   pl.BlockSpec((tm, width), lambda i: (i, 0)),
                   pl.BlockSpec((1, DA_HEADS * DA_VT_ROWS, tm), lambda i: (i // tiles, 0, i % tiles))),
        out_shape=(jax.ShapeDtypeStruct((batch, width, seq_len), BF16),
                   jax.ShapeDtypeStruct((t, width), BF16),
                   jax.ShapeDtypeStruct((batch, DA_HEADS * DA_VT_ROWS, seq_len), BF16)),
        compiler_params=_params("parallel"),
        name="attn_proj",
    )(x2, g.reshape(1, d), w_bf16)


def _bucket_band(tile):
    r = np.arange(3 * tile)[:, None]
    i = np.arange(tile)[None, :]
    dist = i + 2 * tile - r
    max_exact = REL_BUCKETS // 2
    d = np.maximum(dist, 1).astype(np.float32)
    large = max_exact + (np.log(d / np.float32(max_exact)) / np.float32(math.log(REL_MAX_DIST / max_exact))
                         * np.float32(REL_BUCKETS - max_exact)).astype(np.int32)
    large = np.minimum(large, REL_BUCKETS - 1)
    bucket = np.where(dist < max_exact, dist, large)
    bucket = np.where(dist < 0, -1, bucket).astype(np.int32)
    assert (bucket[:tile] == REL_BUCKETS - 1).all()
    return bucket


def _bias_band_kernel(rel_ref, bucket_ref, o_ref):
    h = pl.program_id(0)
    bucket = bucket_ref[...]
    acc = jnp.full(bucket.shape, NEG, F32)
    for b in range(REL_BUCKETS):
        acc = jnp.where(bucket == b, rel_ref[b, h] * LOG2E, acc)
    o_ref[0] = acc


def _bias_band(rel_bias, tile):
    bucket = jnp.asarray(_bucket_band(tile))
    return pl.pallas_call(
        _bias_band_kernel,
        grid=(DA_HEADS,),
        in_specs=[pl.BlockSpec(memory_space=pltpu.SMEM),
                  _const_spec((3 * tile, tile))],
        out_specs=pl.BlockSpec((1, 3 * tile, tile), lambda h: (h, 0, 0)),
        out_shape=jax.ShapeDtypeStruct((DA_HEADS, 3 * tile, tile), F32),
        compiler_params=_params("arbitrary"),
        name="bias_band",
    )(rel_bias, bucket)


def _diff_attn_kernel(lam_ref, qt_ref, k_ref, vt_ref, band_ref, sub_ref, o_ref,
                      qz_ref, m_ref, acc_ref, sb_ref, cm_ref, *, tile, n_tiles, lambda_init):
    head_cols = [slice(h * DA_V_DIM, (h + 1) * DA_V_DIM) for h in range(DA_HEADS)]
    depth = ATTN_PIPE_DEPTH
    slots = sb_ref.shape[0]
    per_iter = ATTN_TILES_PER_ITER

    def tile_span(j):
        return pl.ds(pl.multiple_of(j * tile, tile), tile)

    def scores(j, h):
        return jnp.dot(k_ref[tile_span(j), head_cols[h]], qz_ref[h], preferred_element_type=F32)

    def drain(i, j, h, s_raw):
        kind = jnp.clip(j - (i - 2), 0, 2)
        bias = band_ref[h, pl.ds(pl.multiple_of(kind * tile, tile), tile), :]
        sb = s_raw + jnp.concatenate([bias, bias], axis=1)
        sb_ref[h % slots] = sb
        cm_ref[h % slots] = jnp.max(sb.reshape(tile // F32_SUBLANES, F32_SUBLANES, 2 * tile), axis=0)

    def consume(j, h):
        m = m_ref[h]
        m_new = jnp.maximum(m, jnp.max(cm_ref[h % slots], axis=0, keepdims=True))
        m_ref[h] = m_new
        p = jnp.exp2((sb_ref[h % slots] - m_new).astype(BF16))
        vt_blk = vt_ref[0, h * DA_VT_ROWS:(h + 1) * DA_VT_ROWS, tile_span(j)]
        return jnp.dot(vt_blk, p, preferred_element_type=F32), jnp.exp2(m - m_new)

    def rescale_add(h, pv, alpha):
        acc_ref[h] = alpha * acc_ref[h] + pv

    def item(j, n):
        return jnp.minimum(j + n // DA_HEADS, n_tiles - 1), n % DA_HEADS

    def run_tiles(i, j0, tiles):
        issued = None
        pending = None
        for n in range(tiles * DA_HEADS):
            jn, hn = item(j0, n + depth)
            nxt = (i, jn, hn, scores(jn, hn))
            pv, alpha = consume(j0 + n // DA_HEADS, n % DA_HEADS)
            if issued is not None:
                drain(*issued)
            if pending is not None:
                rescale_add(*pending)
            issued, pending = nxt, (n % DA_HEADS, pv, alpha)
        drain(*issued)
        rescale_add(*pending)

    def open_query_tile(i):
        sub = lax.broadcasted_iota(jnp.int32, (DA_V_DIM, tile), 0)
        first_map = (sub < DA_HEAD_DIM).astype(F32)
        for h in range(DA_HEADS):
            q_t = qt_ref[0, head_cols[h], tile_span(i)].astype(F32)
            qz_ref[h] = jnp.concatenate([q_t * first_map, q_t * (1.0 - first_map)], axis=1).astype(BF16)
        for n in range(depth):
            drain(i, 0, n, scores(0, n))

    lp = lam_ref[...]
    lam = (jnp.exp(jnp.sum(lp[0:1] * lp[1:2], axis=-1, keepdims=True))
           - jnp.exp(jnp.sum(lp[2:3] * lp[3:4], axis=-1, keepdims=True)) + lambda_init)
    sub_w = sub_ref[...] * (1.0 - lambda_init)

    def query_tile(i, carry):
        m_ref[...] = jnp.full(m_ref.shape, NEG, F32)
        acc_ref[...] = jnp.zeros(acc_ref.shape, F32)

        n_key_tiles = i + 1

        def multi_tile_body(jj, c):
            run_tiles(i, jj * per_iter, per_iter)
            return c

        lax.fori_loop(0, n_key_tiles // per_iter, multi_tile_body, 0)
        for r in range(1, per_iter):
            @pl.when(n_key_tiles % per_iter == r)
            def _(r=r):
                run_tiles(i, n_key_tiles - r, r)

        for h in range(DA_HEADS):
            inv_l = 1.0 / acc_ref[h, DA_V_DIM:DA_V_DIM + 1, :]
            o = (acc_ref[h, 0:DA_V_DIM, :tile] * inv_l[:, :tile]
                 - acc_ref[h, 0:DA_V_DIM, tile:] * (lam * inv_l[:, tile:]))
            o = o * lax.rsqrt(jnp.mean(o * o, axis=0, keepdims=True) + EPS)
            o_ref[0, tile_span(i), head_cols[h]] = (jnp.transpose(o) * sub_w).astype(o_ref.dtype)
        open_query_tile(jnp.minimum(i + 1, n_tiles - 1))
        return carry

    open_query_tile(0)
    lax.fori_loop(0, n_tiles, query_tile, 0)


def _diff_attn(q_t, k, v_t, band, lam_params, subln, lambda_init):
    b, width, s = q_t.shape
    tile = ATTN_TILE
    assert s % tile == 0 and width == DA_HEADS * DA_V_DIM
    kern = functools.partial(_diff_attn_kernel, tile=tile, n_tiles=s // tile, lambda_init=lambda_init)
    return pl.pallas_call(
        kern,
        grid=(b,),
        in_specs=[_const_spec(lam_params.shape),
                  pl.BlockSpec((1, width, s), lambda bi: (bi, 0, 0)),
                  pl.BlockSpec((s, width), lambda bi: (bi, 0)),
                  pl.BlockSpec((1, DA_HEADS * DA_VT_ROWS, s), lambda bi: (bi, 0, 0)),
                  pl.BlockSpec(band.shape, lambda bi: (0, 0, 0), pipeline_mode=pl.Buffered(1)),
                  _const_spec((1, DA_V_DIM))],
        out_specs=pl.BlockSpec((1, s, width), lambda bi: (bi, 0, 0)),
        out_shape=jax.ShapeDtypeStruct((b, s, width), BF16),
        scratch_shapes=[pltpu.VMEM((DA_HEADS, DA_V_DIM, 2 * tile), BF16),
                        pltpu.VMEM((DA_HEADS, 1, 2 * tile), F32),
                        pltpu.VMEM((DA_HEADS, DA_VT_ROWS, 2 * tile), F32),
                        pltpu.VMEM((ATTN_RING, tile, 2 * tile), F32),
                        pltpu.VMEM((ATTN_RING, F32_SUBLANES, 2 * tile), F32)],
        compiler_params=_params("parallel"),
        name="diff_attn",
    )(lam_params, q_t, k, v_t, band, subln.reshape(1, DA_V_DIM))


def _out_mlp_kernel(a_ref, h_ref, wo_ref, g_ref, w1_ref, w2_ref, gf_ref, o_ref, u_ref,
                    *, ff_chunk, final_norm):
    h1 = h_ref[...] + jnp.dot(a_ref[...], wo_ref[...], preferred_element_type=F32)
    xn = _rmsnorm_rows(h1, g_ref[...]).astype(BF16)
    d_ff = w1_ref.shape[1]
    for c in range(d_ff // ff_chunk):
        sl = slice(c * ff_chunk, (c + 1) * ff_chunk)
        u = jnp.maximum(jnp.dot(xn, w1_ref[:, sl], preferred_element_type=F32), 0.0)
        u_ref[:, sl] = (u * u).astype(BF16)
    h2 = h1 + jnp.dot(u_ref[...], w2_ref[...], preferred_element_type=F32)
    if final_norm:
        h2 = _rmsnorm_rows(h2, gf_ref[...])
    o_ref[...] = h2


def _out_mlp(a2, h2, wo, g, w1, w2, gf, final_norm):
    t, d = h2.shape
    d_ff = w1.shape[1]
    kern = functools.partial(_out_mlp_kernel, ff_chunk=MLP_FF_CHUNK, final_norm=final_norm)
    single = pl.Buffered(1)
    return pl.pallas_call(
        kern,
        grid=(t // ROW_TILE,),
        in_specs=[pl.BlockSpec((ROW_TILE, a2.shape[1]), lambda i: (i, 0)),
                  pl.BlockSpec((ROW_TILE, d), lambda i: (i, 0)),
                  pl.BlockSpec(wo.shape, lambda i: (0, 0), pipeline_mode=single),
                  _const_spec((1, d)),
                  pl.BlockSpec(w1.shape, lambda i: (0, 0), pipeline_mode=single),
                  pl.BlockSpec(w2.shape, lambda i: (0, 0), pipeline_mode=single),
                  _const_spec((1, d))],
        out_specs=pl.BlockSpec((ROW_TILE, d), lambda i: (i, 0)),
        out_shape=jax.ShapeDtypeStruct((t, d), F32),
        scratch_shapes=[pltpu.VMEM((ROW_TILE, d_ff), BF16)],
        compiler_params=_params("parallel"),
        name="out_mlp_final" if final_norm else "out_mlp",
    )(a2, h2, wo, g.reshape(1, d), w1, w2, gf.reshape(1, d))


def _log_sigmoid(x):
    return -(jnp.maximum(-x, 0.0) + jnp.log(1.0 + jnp.exp(-jnp.abs(x))))


def _mlstm_layer_kernel(x_ref, xh_ref, g_ref, w_ref, cw_ref, cb_ref, gb_ref, hn_ref, o_ref,
                        cs_ref, qt_ref, k_ref, vt_ref, og_ref, gcol_ref, grow_ref, s_ref, m_ref,
                        *, tiles_per_seq, q_scale, qk_w, v_w):
    i = pl.program_id(0)
    tm = x_ref.shape[0]
    nc = ML_PROJ_CHUNK
    L = ML_CHUNK
    seq_start = (i % tiles_per_seq) == 0

    @pl.when(seq_start)
    def _():
        s_ref[...] = jnp.zeros_like(s_ref)
        m_ref[...] = jnp.zeros_like(m_ref)

    g = g_ref[...]
    xn = _rmsnorm_rows(x_ref[...], g).astype(BF16)
    xh = _rmsnorm_rows(xh_ref[...], g).astype(BF16)
    ones = jnp.ones((ML_VT_ROWS - ML_V_DIM, tm), BF16)
    n_chunks = (qk_w + 2 * v_w) // nc
    gate_chunk = n_chunks

    def proj_matmul(c):
        if c == gate_chunk:
            return jnp.dot(xn, w_ref[:, c * nc:c * nc + ML_GATE_PAD], preferred_element_type=F32)
        return jnp.dot(xn, w_ref[:, c * nc:(c + 1) * nc], preferred_element_type=F32)

    def proj_epilogue(c, y):
        col0 = c * nc
        cols = slice(col0, col0 + nc)
        if c == gate_chunk:
            gates = y + gb_ref[...]
            lane = lax.broadcasted_iota(jnp.int32, gates.shape, 1)
            gates = jnp.where(lane < ML_HEADS, gates, _log_sigmoid(gates))
            gcol_ref[...] = gates
            grow_ref[...] = jnp.transpose(gates)[0:grow_ref.shape[0], :]
        elif col0 < qk_w:
            slot = c % 2
            halo = jnp.dot(xh, w_ref[:, cols], preferred_element_type=F32)
            cs_ref[slot, 0:HALO, :] = jnp.where(seq_start, 0.0, halo)
            cs_ref[slot, HALO:HALO + tm, :] = y
            conv = cb_ref[:, cols]
            for j in range(CONV_WIDTH):
                off = HALO - (CONV_WIDTH - 1) + j
                conv = conv + cw_ref[j:j + 1, cols] * cs_ref[slot, off:off + tm, :]
            act = conv * jax.nn.sigmoid(conv)
            if col0 < qk_w // 2:
                qt_ref[cols, :] = jnp.transpose(act * q_scale).astype(BF16)
            else:
                k_ref[:, col0 - qk_w // 2:col0 - qk_w // 2 + nc] = act.astype(BF16)
        elif col0 < qk_w + v_w:
            r0 = ((col0 - qk_w) // ML_V_DIM) * ML_VT_ROWS
            vt_ref[r0:r0 + ML_V_DIM, :] = jnp.transpose(y).astype(BF16)
            vt_ref[r0 + ML_V_DIM:r0 + ML_VT_ROWS, :] = ones
        else:
            o0 = col0 - qk_w - v_w
            og_ref[:, o0:o0 + nc] = jax.nn.sigmoid(y).astype(BF16)

    row = lax.broadcasted_iota(jnp.int32, (L, L), 0)
    col = lax.broadcasted_iota(jnp.int32, (L, L), 1)
    key_le_query = row <= col

    def qk_blocks(c, h):
        pos = slice(c * L, (c + 1) * L)
        return (qt_ref[h * ML_QK_DIM:(h + 1) * ML_QK_DIM, pos],
                k_ref[pos, h * ML_QK_DIM:(h + 1) * ML_QK_DIM])

    def rec_matmuls(c, h):
        q_t, kh = qk_blocks(c, h)
        s_t = jnp.dot(kh, q_t, preferred_element_type=F32)
        sq = jnp.dot(s_ref[h].astype(BF16), q_t, preferred_element_type=F32)
        return s_t, sq

    def rec_head(c, h, s_t, sq):
        pos = slice(c * L, (c + 1) * L)
        _, kh = qk_blocks(c, h)
        vcols = slice(h * ML_V_DIM, (h + 1) * ML_V_DIM)
        v_t = vt_ref[h * ML_VT_ROWS:(h + 1) * ML_VT_ROWS, pos]
        ig_r = grow_ref[h:h + 1, pos]
        lf_r = grow_ref[ML_HEADS + h:ML_HEADS + h + 1, pos]
        ig_c = gcol_ref[pos, h:h + 1]
        lf_c = gcol_ref[pos, ML_HEADS + h:ML_HEADS + h + 1]
        m_prev = m_ref[h]

        b_r = jnp.sum(jnp.where(key_le_query, lf_c, 0.0), axis=0, keepdims=True)
        b_c = jnp.sum(jnp.where(key_le_query, 0.0, lf_r), axis=1, keepdims=True) + lf_c
        b_last = jnp.sum(lf_r, axis=1, keepdims=True)

        d_t = jnp.where(key_le_query, b_r - (b_c - ig_c), NEG)
        m_inter = b_r + m_prev
        m_j = jnp.maximum(m_inter, jnp.max(d_t, axis=0, keepdims=True))
        sc_t = s_t * jnp.exp(d_t - m_j)
        inter = jnp.exp(m_inter - m_j)
        tot = jnp.dot(v_t, sc_t.astype(BF16), preferred_element_type=F32) + inter * sq
        den = tot[ML_V_DIM:ML_V_DIM + 1, :]
        h_t = tot[0:ML_V_DIM, :] * (1.0 / jnp.maximum(jnp.abs(den), jnp.exp(-m_j)))
        y_t = h_t * lax.rsqrt(jnp.mean(h_t * h_t, axis=0, keepdims=True) + EPS)
        y = jnp.transpose(y_t) * hn_ref[:, vcols]
        o_ref[pos, vcols] = (og_ref[pos, vcols].astype(F32) * y).astype(o_ref.dtype)

        a_r = b_last - b_r + ig_r
        m_new = jnp.maximum(b_last + m_prev, jnp.max(a_r, axis=1, keepdims=True))
        w_r = jnp.exp(a_r - m_new)
        decay = jnp.exp(b_last + m_prev - m_new)
        wv_t = (v_t.astype(F32) * w_r).astype(BF16)
        s_ref[h] = decay * s_ref[h] + jnp.dot(wv_t, kh, preferred_element_type=F32)
        m_ref[h] = m_new

    qk_chunks = list(range(qk_w // nc))
    v_chunks = list(range(qk_w // nc, (qk_w + v_w) // nc))
    o_chunks = list(range((qk_w + v_w) // nc, n_chunks))
    first = [gate_chunk]
    while qk_chunks or v_chunks:
        first += v_chunks[:1] + qk_chunks[:1]
        qk_chunks, v_chunks = qk_chunks[1:], v_chunks[1:]
    rec_items = [(c, h) for c in range(tm // L) for h in range(ML_HEADS)]
    assert len(o_chunks) == ML_HEADS and ML_PROJ_AHEAD < ML_HEADS
    stages = [("proj", c) for c in first]
    for h in range(ML_HEADS):
        stages += [("proj", o_chunks[h]), ("rec", rec_items[h])]
    stages += [("rec", it) for it in rec_items[ML_HEADS:]]

    def issue(stage):
        kind, arg = stage
        return proj_matmul(arg) if kind == "proj" else rec_matmuls(*arg)

    ahead = ML_PROJ_AHEAD
    n_proj = len(first)
    issued = {}
    for n in range(min(ahead, n_proj)):
        issued[n] = issue(stages[n])
    for n, stage in enumerate(stages):
        nxt = n + ahead
        if nxt < len(stages) and nxt not in issued:
            if stages[nxt][0] == "proj" or n >= n_proj - 1:
                issued[nxt] = issue(stages[nxt])
        if n not in issued:
            issued[n] = issue(stage)
        kind, arg = stage
        if kind == "proj":
            proj_epilogue(arg, issued.pop(n))
        else:
            rec_head(*arg, *issued.pop(n))


def _mlstm_layer(x2, xnorm_g, w_all, conv_w, conv_b, gate_b, head_norm, batch, seq_len):
    t, d = x2.shape
    tm = ROW_TILE
    qk_w = 2 * ML_HEADS * ML_QK_DIM
    v_w = ML_HEADS * ML_V_DIM
    assert seq_len % tm == 0 and tm % HALO == 0 and tm % ML_CHUNK == 0 and ML_PROJ_CHUNK == ML_V_DIM
    assert w_all.shape[1] == qk_w + 2 * v_w + ML_GATE_PAD
    tiles = seq_len // tm
    kern = functools.partial(_mlstm_layer_kernel, tiles_per_seq=tiles, q_scale=ML_QK_DIM ** -0.5,
                             qk_w=qk_w, v_w=v_w)
    halo_blocks = tm // HALO
    return pl.pallas_call(
        kern,
        grid=(t // tm,),
        in_specs=[pl.BlockSpec((tm, d), lambda i: (i, 0)),
                  pl.BlockSpec((HALO, d), lambda i: (jnp.maximum(i * halo_blocks - 1, 0), 0)),
                  _const_spec((1, d)),
                  pl.BlockSpec(w_all.shape, lambda i: (0, 0), pipeline_mode=pl.Buffered(1)),
                  _const_spec(conv_w.shape),
                  _const_spec((1, qk_w)),
                  _const_spec((1, ML_GATE_PAD)),
                  _const_spec((1, v_w))],
        out_specs=pl.BlockSpec((tm, v_w), lambda i: (i, 0)),
        out_shape=jax.ShapeDtypeStruct((t, v_w), BF16),
        scratch_shapes=[pltpu.VMEM((2, HALO + tm, ML_PROJ_CHUNK), F32),
                        pltpu.VMEM((qk_w // 2, tm), BF16),
                        pltpu.VMEM((tm, qk_w // 2), BF16),
                        pltpu.VMEM((ML_HEADS * ML_VT_ROWS, tm), BF16),
                        pltpu.VMEM((tm, v_w), BF16),
                        pltpu.VMEM((tm, ML_GATE_PAD), F32),
                        pltpu.VMEM((2 * ML_HEADS, tm), F32),
                        pltpu.VMEM((ML_HEADS, ML_VT_ROWS, ML_QK_DIM), F32),
                        pltpu.VMEM((ML_HEADS, 1, 1), F32)],
        compiler_params=_params("arbitrary"),
        name="mlstm_layer",
    )(x2, x2, xnorm_g.reshape(1, d), w_all, conv_w, conv_b.reshape(1, qk_w),
      gate_b.reshape(1, ML_GATE_PAD), head_norm.reshape(1, v_w))


def kernel(x, rel_bias, attn_norm, attn_w_in, attn_lambda_q1, attn_lambda_k1, attn_lambda_q2,
           attn_lambda_k2, attn_subln, attn_w_out, mlstm_norm, mlstm_w_in, mlstm_b_i, mlstm_b_f,
           mlstm_conv_w, mlstm_conv_b, mlstm_head_norm, mlstm_w_out, mlp_norm, mlp_w1, mlp_w2,
           final_norm):
    batch, seq_len, d = x.shape
    t = batch * seq_len
    h = x.reshape(t, d)

    lambda_init = 0.8 - 0.6 * math.exp(-0.3 * 0)
    q_t, k, v_t = _attn_proj(h, attn_norm[0], attn_w_in[0].astype(BF16), batch, seq_len)
    band = _bias_band(rel_bias, ATTN_TILE)
    lam_params = jnp.stack([attn_lambda_q1[0], attn_lambda_k1[0],
                            attn_lambda_q2[0], attn_lambda_k2[0]]).astype(F32)
    attn = _diff_attn(q_t, k, v_t, band, lam_params, attn_subln[0], lambda_init)
    h = _out_mlp(attn.reshape(t, -1), h, attn_w_out[0].astype(BF16), mlp_norm[0],
                 mlp_w1[0].astype(BF16), mlp_w2[0].astype(BF16), final_norm, final_norm=False)

    n_gates = 2 * ML_HEADS
    w_all = jnp.pad(mlstm_w_in[0], ((0, 0), (0, ML_GATE_PAD - n_gates))).astype(BF16)
    gate_b = jnp.pad(jnp.concatenate([mlstm_b_i[0], mlstm_b_f[0]]).astype(F32),
                     (0, ML_GATE_PAD - n_gates))
    mix = _mlstm_layer(h, mlstm_norm[0], w_all, mlstm_conv_w[0], mlstm_conv_b[0], gate_b,
                       mlstm_head_norm[0], batch, seq_len)
    out = _out_mlp(mix, h, mlstm_w_out[0].astype(BF16), mlp_norm[1],
                   mlp_w1[1].astype(BF16), mlp_w2[1].astype(BF16), final_norm, final_norm=True)
    return out.reshape(batch, seq_len, d)
```

```python
import functools
import math

import numpy as np
import jax
import jax.numpy as jnp
from jax import lax
from jax.experimental import pallas as pl
from jax.experimental.pallas import tpu as pltpu

F32 = jnp.float32
BF16 = jnp.bfloat16

EPS = 1e-6
NEG = -1e30

LANES = 128
F32_SUBLANES = 8
BF16_SUBLANES = 16
VMEM_LIMIT = 56 * 1024 * 1024

DA_HEADS = 8
DA_HEAD_DIM = 64
DA_V_DIM = 2 * DA_HEAD_DIM
REL_BUCKETS = 32
REL_MAX_DIST = 128
DA_VT_ROWS = DA_V_DIM + BF16_SUBLANES
LOG2E = math.log2(math.e)

ML_HEADS = 4
ML_QK_DIM = 128
ML_V_DIM = 256
CONV_WIDTH = 4
ML_GATE_PAD = LANES
ML_VT_ROWS = ML_V_DIM + BF16_SUBLANES

ROW_TILE = 512
ATTN_PROJ_CHUNK = 512
MLP_FF_CHUNK = 1024
MLP_ROW_TILE = 1024
ATTN_TILE = 256
ML_CHUNK = 256
ML_PROJ_CHUNK = 256
ML_PROJ_AHEAD = 2
ATTN_PIPE_DEPTH = 4
ATTN_RING = 4
ATTN_TILES_PER_ITER = 2
HALO = BF16_SUBLANES


def _params(*sem):
    return pltpu.CompilerParams(dimension_semantics=sem, vmem_limit_bytes=VMEM_LIMIT)


def _const_spec(shape):
    nd = len(shape)
    return pl.BlockSpec(shape, lambda *_: (0,) * nd)


def _rmsnorm_rows(x, g):
    return x * lax.rsqrt(jnp.mean(x * x, axis=-1, keepdims=True) + EPS) * g


def _attn_proj_kernel(x_ref, g_ref, w_ref, qt_ref, k_ref, vt_ref, *, n_chunk, width, q_scale):
    xn = _rmsnorm_rows(x_ref[...], g_ref[...]).astype(BF16)
    tm = x_ref.shape[0]
    per_part = width // n_chunk
    heads_per_chunk = n_chunk // DA_V_DIM
    ones = jnp.ones((DA_VT_ROWS - DA_V_DIM, tm), BF16)
    for c in range(3 * per_part):
        y = jnp.dot(xn, w_ref[:, c * n_chunk:(c + 1) * n_chunk], preferred_element_type=F32)
        part, pc = divmod(c, per_part)
        if part == 0:
            qt_ref[0, pc * n_chunk:(pc + 1) * n_chunk, :] = jnp.transpose(y * q_scale).astype(BF16)
        elif part == 1:
            k_ref[:, pc * n_chunk:(pc + 1) * n_chunk] = y.astype(BF16)
        else:
            y_t = jnp.transpose(y).astype(BF16)
            for hh in range(heads_per_chunk):
                r0 = (pc * heads_per_chunk + hh) * DA_VT_ROWS
                vt_ref[0, r0:r0 + DA_V_DIM, :] = y_t[hh * DA_V_DIM:(hh + 1) * DA_V_DIM]
                vt_ref[0, r0 + DA_V_DIM:r0 + DA_VT_ROWS, :] = ones


def _attn_proj(x2, g, w_bf16, batch, seq_len):
    t, d = x2.shape
    width = w_bf16.shape[1] // 3
    n_chunk = ATTN_PROJ_CHUNK
    tm = ROW_TILE
    assert seq_len % tm == 0 and width % n_chunk == 0 and width == DA_HEADS * DA_V_DIM
    tiles = seq_len // tm
    kern = functools.partial(_attn_proj_kernel, n_chunk=n_chunk, width=width,
                             q_scale=DA_HEAD_DIM ** -0.5 * LOG2E)
    return pl.pallas_call(
        kern,
        grid=(t // tm,),
        in_specs=[pl.BlockSpec((tm, d), lambda i: (i, 0)),
                  _const_spec((1, d)),
                  _const_spec((d, 3 * width))],
        out_specs=(pl.BlockSpec((1, width, tm), lambda i: (i // tiles, 0, i % tiles)),
                   pl.BlockSpec((tm, width), lambda i: (i, 0)),
                   pl.BlockSpec((1, DA_HEADS * DA_VT_ROWS, tm), lambda i: (i // tiles, 0, i % tiles))),
        out_shape=(jax.ShapeDtypeStruct((batch, width, seq_len), BF16),
                   jax.ShapeDtypeStruct((t, width), BF16),
                   jax.ShapeDtypeStruct((batch, DA_HEADS * DA_VT_ROWS, seq_len), BF16)),
        compiler_params=_params("parallel"),
        name="attn_proj",
    )(x2, g.reshape(1, d), w_bf16)


def _bucket_band(tile):
    r = np.arange(3 * tile)[:, None]
    i = np.arange(tile)[None, :]
    dist = i + 2 * tile - r
    max_exact = REL_BUCKETS // 2
    d = np.maximum(dist, 1).astype(np.float32)
    large = max_exact + (np.log(d / np.float32(max_exact)) / np.float32(math.log(REL_MAX_DIST / max_exact))
                         * np.float32(REL_BUCKETS - max_exact)).astype(np.int32)
    large = np.minimum(large, REL_BUCKETS - 1)
    bucket = np.where(dist < max_exact, dist, large)
    bucket = np.where(dist < 0, -1, bucket).astype(np.int32)
    assert (bucket[:tile] == REL_BUCKETS - 1).all()
    return bucket


def _bias_band_kernel(rel_ref, bucket_ref, o_ref):
    h = pl.program_id(0)
    bucket = bucket_ref[...]
    acc = jnp.full(bucket.shape, NEG, F32)
    for b in range(REL_BUCKETS):
        acc = jnp.where(bucket == b, rel_ref[b, h] * LOG2E, acc)
    o_ref[0] = acc


def _bias_band(rel_bias, tile):
    bucket = jnp.asarray(_bucket_band(tile))
    return pl.pallas_call(
        _bias_band_kernel,
        grid=(DA_HEADS,),
        in_specs=[pl.BlockSpec(memory_space=pltpu.SMEM),
                  _const_spec((3 * tile, tile))],
        out_specs=pl.BlockSpec((1, 3 * tile, tile), lambda h: (h, 0, 0)),
        out_shape=jax.ShapeDtypeStruct((DA_HEADS, 3 * tile, tile), F32),
        compiler_params=_params("arbitrary"),
        name="bias_band",
    )(rel_bias, bucket)


def _diff_attn_kernel(lam_ref, qt_ref, k_ref, vt_ref, band_ref, sub_ref, o_ref,
                      qz_ref, m_ref, acc_ref, sb_ref, cm_ref, *, tile, n_tiles, lambda_init):
    head_cols = [slice(h * DA_V_DIM, (h + 1) * DA_V_DIM) for h in range(DA_HEADS)]
    depth = ATTN_PIPE_DEPTH
    slots = sb_ref.shape[0]
    per_iter = ATTN_TILES_PER_ITER

    def tile_span(j):
        return pl.ds(pl.multiple_of(j * tile, tile), tile)

    def scores(j, h):
        return jnp.dot(k_ref[tile_span(j), head_cols[h]], qz_ref[h], preferred_element_type=F32)

    def drain(i, j, h, s_raw):
        kind = jnp.clip(j - (i - 2), 0, 2)
        bias = band_ref[h, pl.ds(pl.multiple_of(kind * tile, tile), tile), :]
        sb = s_raw + jnp.concatenate([bias, bias], axis=1)
        sb_ref[h % slots] = sb
        cm_ref[h % slots] = jnp.max(sb.reshape(tile // F32_SUBLANES, F32_SUBLANES, 2 * tile), axis=0)

    def consume(j, h):
        m = m_ref[h]
        m_new = jnp.maximum(m, jnp.max(cm_ref[h % slots], axis=0, keepdims=True))
        m_ref[h] = m_new
        p = jnp.exp2((sb_ref[h % slots] - m_new).astype(BF16))
        vt_blk = vt_ref[0, h * DA_VT_ROWS:(h + 1) * DA_VT_ROWS, tile_span(j)]
        return jnp.dot(vt_blk, p, preferred_element_type=F32), jnp.exp2(m - m_new)

    def rescale_add(h, pv, alpha):
        acc_ref[h] = alpha * acc_ref[h] + pv

    def item(j, n):
        return jnp.minimum(j + n // DA_HEADS, n_tiles - 1), n % DA_HEADS

    def run_tiles(i, j0, tiles):
        issued = None
        pending = None
        for n in range(tiles * DA_HEADS):
            jn, hn = item(j0, n + depth)
            nxt = (i, jn, hn, scores(jn, hn))
            pv, alpha = consume(j0 + n // DA_HEADS, n % DA_HEADS)
            if issued is not None:
                drain(*issued)
            if pending is not None:
                rescale_add(*pending)
            issued, pending = nxt, (n % DA_HEADS, pv, alpha)
        drain(*issued)
        rescale_add(*pending)

    def open_query_tile(i):
        sub = lax.broadcasted_iota(jnp.int32, (DA_V_DIM, tile), 0)
        first_map = (sub < DA_HEAD_DIM).astype(F32)
        for h in range(DA_HEADS):
            q_t = qt_ref[0, head_cols[h], tile_span(i)].astype(F32)
            qz_ref[h] = jnp.concatenate([q_t * first_map, q_t * (1.0 - first_map)], axis=1).astype(BF16)
        for n in range(depth):
            drain(i, 0, n, scores(0, n))

    lp = lam_ref[...]
    lam = (jnp.exp(jnp.sum(lp[0:1] * lp[1:2], axis=-1, keepdims=True))
           - jnp.exp(jnp.sum(lp[2:3] * lp[3:4], axis=-1, keepdims=True)) + lambda_init)
    sub_w = sub_ref[...] * (1.0 - lambda_init)

    def query_tile(i, carry):
        m_ref[...] = jnp.full(m_ref.shape, NEG, F32)
        acc_ref[...] = jnp.zeros(acc_ref.shape, F32)

        n_key_tiles = i + 1

        def multi_tile_body(jj, c):
            run_tiles(i, jj * per_iter, per_iter)
            return c

        lax.fori_loop(0, n_key_tiles // per_iter, multi_tile_body, 0)
        for r in range(1, per_iter):
            @pl.when(n_key_tiles % per_iter == r)
            def _(r=r):
                run_tiles(i, n_key_tiles - r, r)

        for h in range(DA_HEADS):
            inv_l = 1.0 / acc_ref[h, DA_V_DIM:DA_V_DIM + 1, :]
            o = (acc_ref[h, 0:DA_V_DIM, :tile] * inv_l[:, :tile]
                 - acc_ref[h, 0:DA_V_DIM, tile:] * (lam * inv_l[:, tile:]))
            o = o * lax.rsqrt(jnp.mean(o * o, axis=0, keepdims=True) + EPS)
            o_ref[0, tile_span(i), head_cols[h]] = (jnp.transpose(o) * sub_w).astype(o_ref.dtype)
        open_query_tile(jnp.minimum(i + 1, n_tiles - 1))
        return carry

    open_query_tile(0)
    lax.fori_loop(0, n_tiles, query_tile, 0)


def _diff_attn(q_t, k, v_t, band, lam_params, subln, lambda_init):
    b, width, s = q_t.shape
    tile = ATTN_TILE
    assert s % tile == 0 and width == DA_HEADS * DA_V_DIM
    kern = functools.partial(_diff_attn_kernel, tile=tile, n_tiles=s // tile, lambda_init=lambda_init)
    return pl.pallas_call(
        kern,
        grid=(b,),
        in_specs=[_const_spec(lam_params.shape),
                  pl.BlockSpec((1, width, s), lambda bi: (bi, 0, 0)),
                  pl.BlockSpec((s, width), lambda bi: (bi, 0)),
                  pl.BlockSpec((1, DA_HEADS * DA_VT_ROWS, s), lambda bi: (bi, 0, 0)),
                  pl.BlockSpec(band.shape, lambda bi: (0, 0, 0), pipeline_mode=pl.Buffered(1)),
                  _const_spec((1, DA_V_DIM))],
        out_specs=pl.BlockSpec((1, s, width), lambda bi: (bi, 0, 0)),
        out_shape=jax.ShapeDtypeStruct((b, s, width), BF16),
        scratch_shapes=[pltpu.VMEM((DA_HEADS, DA_V_DIM, 2 * tile), BF16),
                        pltpu.VMEM((DA_HEADS, 1, 2 * tile), F32),
                        pltpu.VMEM((DA_HEADS, DA_VT_ROWS, 2 * tile), F32),
                        pltpu.VMEM((ATTN_RING, tile, 2 * tile), F32),
                        pltpu.VMEM((ATTN_RING, F32_SUBLANES, 2 * tile), F32)],
        compiler_params=_params("parallel"),
        name="diff_attn",
    )(lam_params, q_t, k, v_t, band, subln.reshape(1, DA_V_DIM))


def _out_mlp_kernel(a_ref, h_ref, wo_ref, g_ref, w1_ref, w2_ref, gf_ref, o_ref, u_ref,
                    *, ff_chunk, final_norm):
    h1 = h_ref[...] + jnp.dot(a_ref[...], wo_ref[...], preferred_element_type=F32)
    xn = _rmsnorm_rows(h1, g_ref[...]).astype(BF16)
    d_ff = w1_ref.shape[1]
    for c in range(d_ff // ff_chunk):
        sl = slice(c * ff_chunk, (c + 1) * ff_chunk)
        u = jnp.maximum(jnp.dot(xn, w1_ref[:, sl], preferred_element_type=F32), 0.0)
        u_ref[:, sl] = (u * u).astype(BF16)
    h2 = h1 + jnp.dot(u_ref[...], w2_ref[...], preferred_element_type=F32)
    if final_norm:
        h2 = _rmsnorm_rows(h2, gf_ref[...])
    o_ref[...] = h2


def _out_mlp(a2, h2, wo, g, w1, w2, gf, final_norm):
    t, d = h2.shape
    d_ff = w1.shape[1]
    kern = functools.partial(_out_mlp_kernel, ff_chunk=MLP_FF_CHUNK, final_norm=final_norm)
    single = pl.Buffered(1)
    return pl.pallas_call(
        kern,
        grid=(t // MLP_ROW_TILE,),
        in_specs=[pl.BlockSpec((MLP_ROW_TILE, a2.shape[1]), lambda i: (i, 0)),
                  pl.BlockSpec((MLP_ROW_TILE, d), lambda i: (i, 0)),
                  pl.BlockSpec(wo.shape, lambda i: (0, 0), pipeline_mode=single),
                  _const_spec((1, d)),
                  pl.BlockSpec(w1.shape, lambda i: (0, 0), pipeline_mode=single),
                  pl.BlockSpec(w2.shape, lambda i: (0, 0), pipeline_mode=single),
                  _const_spec((1, d))],
        out_specs=pl.BlockSpec((MLP_ROW_TILE, d), lambda i: (i, 0)),
        out_shape=jax.ShapeDtypeStruct((t, d), F32),
        scratch_shapes=[pltpu.VMEM((MLP_ROW_TILE, d_ff), BF16)],
        compiler_params=_params("parallel"),
        name="out_mlp_final" if final_norm else "out_mlp",
    )(a2, h2, wo, g.reshape(1, d), w1, w2, gf.reshape(1, d))


def _log_sigmoid(x):
    return -(jnp.maximum(-x, 0.0) + jnp.log(1.0 + jnp.exp(-jnp.abs(x))))


def _mlstm_layer_kernel(x_ref, xh_ref, g_ref, w_ref, cw_ref, cb_ref, gb_ref, hn_ref, o_ref,
                        cs_ref, qt_ref, k_ref, vt_ref, og_ref, gcol_ref, grow_ref, s_ref, m_ref,
                        *, tiles_per_seq, q_scale, qk_w, v_w):
    i = pl.program_id(0)
    tm = x_ref.shape[0]
    nc = ML_PROJ_CHUNK
    L = ML_CHUNK
    seq_start = (i % tiles_per_seq) == 0

    @pl.when(seq_start)
    def _():
        s_ref[...] = jnp.zeros_like(s_ref)
        m_ref[...] = jnp.zeros_like(m_ref)

    g = g_ref[...]
    xn = _rmsnorm_rows(x_ref[...], g).astype(BF16)
    xh = _rmsnorm_rows(xh_ref[...], g).astype(BF16)
    ones = jnp.ones((ML_VT_ROWS - ML_V_DIM, tm), BF16)
    n_chunks = (qk_w + 2 * v_w) // nc
    gate_chunk = n_chunks

    def proj_matmul(c):
        if c == gate_chunk:
            return jnp.dot(xn, w_ref[:, c * nc:c * nc + ML_GATE_PAD], preferred_element_type=F32)
        return jnp.dot(xn, w_ref[:, c * nc:(c + 1) * nc], preferred_element_type=F32)

    def proj_epilogue(c, y):
        col0 = c * nc
        cols = slice(col0, col0 + nc)
        if c == gate_chunk:
            gates = y + gb_ref[...]
            lane = lax.broadcasted_iota(jnp.int32, gates.shape, 1)
            gates = jnp.where(lane < ML_HEADS, gates, _log_sigmoid(gates))
            gcol_ref[...] = gates
            grow_ref[...] = jnp.transpose(gates)[0:grow_ref.shape[0], :]
        elif col0 < qk_w:
            slot = c % 2
            halo = jnp.dot(xh, w_ref[:, cols], preferred_element_type=F32)
            cs_ref[slot, 0:HALO, :] = jnp.where(seq_start, 0.0, halo)
            cs_ref[slot, HALO:HALO + tm, :] = y
            conv = cb_ref[:, cols]
            for j in range(CONV_WIDTH):
                off = HALO - (CONV_WIDTH - 1) + j
                conv = conv + cw_ref[j:j + 1, cols] * cs_ref[slot, off:off + tm, :]
            act = conv * jax.nn.sigmoid(conv)
            if col0 < qk_w // 2:
                qt_ref[cols, :] = jnp.transpose(act * q_scale).astype(BF16)
            else:
                k_ref[:, col0 - qk_w // 2:col0 - qk_w // 2 + nc] = act.astype(BF16)
        elif col0 < qk_w + v_w:
            r0 = ((col0 - qk_w) // ML_V_DIM) * ML_VT_ROWS
            vt_ref[r0:r0 + ML_V_DIM, :] = jnp.transpose(y).astype(BF16)
            vt_ref[r0 + ML_V_DIM:r0 + ML_VT_ROWS, :] = ones
        else:
            o0 = col0 - qk_w - v_w
            og_ref[:, o0:o0 + nc] = jax.nn.sigmoid(y).astype(BF16)

    row = lax.broadcasted_iota(jnp.int32, (L, L), 0)
    col = lax.broadcasted_iota(jnp.int32, (L, L), 1)
    key_le_query = row <= col

    def qk_blocks(c, h):
        pos = slice(c * L, (c + 1) * L)
        return (qt_ref[h * ML_QK_DIM:(h + 1) * ML_QK_DIM, pos],
                k_ref[pos, h * ML_QK_DIM:(h + 1) * ML_QK_DIM])

    def rec_matmuls(c, h):
        q_t, kh = qk_blocks(c, h)
        s_t = jnp.dot(kh, q_t, preferred_element_type=F32)
        sq = jnp.dot(s_ref[h].astype(BF16), q_t, preferred_element_type=F32)
        return s_t, sq

    def rec_head(c, h, s_t, sq):
        pos = slice(c * L, (c + 1) * L)
        _, kh = qk_blocks(c, h)
        vcols = slice(h * ML_V_DIM, (h + 1) * ML_V_DIM)
        v_t = vt_ref[h * ML_VT_ROWS:(h + 1) * ML_VT_ROWS, pos]
        ig_r = grow_ref[h:h + 1, pos]
        lf_r = grow_ref[ML_HEADS + h:ML_HEADS + h + 1, pos]
        ig_c = gcol_ref[pos, h:h + 1]
        lf_c = gcol_ref[pos, ML_HEADS + h:ML_HEADS + h + 1]
        m_prev = m_ref[h]

        b_r = jnp.sum(jnp.where(key_le_query, lf_c, 0.0), axis=0, keepdims=True)
        b_c = jnp.sum(jnp.where(key_le_query, 0.0, lf_r), axis=1, keepdims=True) + lf_c
        b_last = jnp.sum(lf_r, axis=1, keepdims=True)

        d_t = jnp.where(key_le_query, b_r - (b_c - ig_c), NEG)
        m_inter = b_r + m_prev
        m_j = jnp.maximum(m_inter, jnp.max(d_t, axis=0, keepdims=True))
        sc_t = s_t * jnp.exp(d_t - m_j)
        inter = jnp.exp(m_inter - m_j)
        tot = jnp.dot(v_t, sc_t.astype(BF16), preferred_element_type=F32) + inter * sq
        den = tot[ML_V_DIM:ML_V_DIM + 1, :]
        h_t = tot[0:ML_V_DIM, :] * (1.0 / jnp.maximum(jnp.abs(den), jnp.exp(-m_j)))
        y_t = h_t * lax.rsqrt(jnp.mean(h_t * h_t, axis=0, keepdims=True) + EPS)
        y = jnp.transpose(y_t) * hn_ref[:, vcols]
        o_ref[pos, vcols] = (og_ref[pos, vcols].astype(F32) * y).astype(o_ref.dtype)

        a_r = b_last - b_r + ig_r
        m_new = jnp.maximum(b_last + m_prev, jnp.max(a_r, axis=1, keepdims=True))
        w_r = jnp.exp(a_r - m_new)
        decay = jnp.exp(b_last + m_prev - m_new)
        wv_t = (v_t.astype(F32) * w_r).astype(BF16)
        s_ref[h] = decay * s_ref[h] + jnp.dot(wv_t, kh, preferred_element_type=F32)
        m_ref[h] = m_new

    qk_chunks = list(range(qk_w // nc))
    v_chunks = list(range(qk_w // nc, (qk_w + v_w) // nc))
    o_chunks = list(range((qk_w + v_w) // nc, n_chunks))
    first = [gate_chunk]
    while qk_chunks or v_chunks:
        first += v_chunks[:1] + qk_chunks[:1]
        qk_chunks, v_chunks = qk_chunks[1:], v_chunks[1:]
    rec_items = [(c, h) for c in range(tm // L) for h in range(ML_HEADS)]
    assert len(o_chunks) == ML_HEADS and ML_PROJ_AHEAD < ML_HEADS
    stages = [("proj", c) for c in first]
    for h in range(ML_HEADS):
        stages += [("proj", o_chunks[h]), ("rec", rec_items[h])]
    stages += [("rec", it) for it in rec_items[ML_HEADS:]]

    def issue(stage):
        kind, arg = stage
        return proj_matmul(arg) if kind == "proj" else rec_matmuls(*arg)

    ahead = ML_PROJ_AHEAD
    n_proj = len(first)
    issued = {}
    for n in range(min(ahead, n_proj)):
        issued[n] = issue(stages[n])
    for n, stage in enumerate(stages):
        nxt = n + ahead
        if nxt < len(stages) and nxt not in issued:
            if stages[nxt][0] == "proj" or n >= n_proj - 1:
                issued[nxt] = issue(stages[nxt])
        if n not in issued:
            issued[n] = issue(stage)
        kind, arg = stage
        if kind == "proj":
            proj_epilogue(arg, issued.pop(n))
        else:
            rec_head(*arg, *issued.pop(n))


def _mlstm_layer(x2, xnorm_g, w_all, conv_w, conv_b, gate_b, head_norm, batch, seq_len):
    t, d = x2.shape
    tm = ROW_TILE
    qk_w = 2 * ML_HEADS * ML_QK_DIM
    v_w = ML_HEADS * ML_V_DIM
    assert seq_len % tm == 0 and tm % HALO == 0 and tm % ML_CHUNK == 0 and ML_PROJ_CHUNK == ML_V_DIM
    assert w_all.shape[1] == qk_w + 2 * v_w + ML_GATE_PAD
    tiles = seq_len // tm
    kern = functools.partial(_mlstm_layer_kernel, tiles_per_seq=tiles, q_scale=ML_QK_DIM ** -0.5,
                             qk_w=qk_w, v_w=v_w)
    halo_blocks = tm // HALO
    return pl.pallas_call(
        kern,
        grid=(t // tm,),
        in_specs=[pl.BlockSpec((tm, d), lambda i: (i, 0)),
                  pl.BlockSpec((HALO, d), lambda i: (jnp.maximum(i * halo_blocks - 1, 0), 0)),
                  _const_spec((1, d)),
                  pl.BlockSpec(w_all.shape, lambda i: (0, 0), pipeline_mode=pl.Buffered(1)),
                  _const_spec(conv_w.shape),
                  _const_spec((1, qk_w)),
                  _const_spec((1, ML_GATE_PAD)),
                  _const_spec((1, v_w))],
        out_specs=pl.BlockSpec((tm, v_w), lambda i: (i, 0)),
        out_shape=jax.ShapeDtypeStruct((t, v_w), BF16),
        scratch_shapes=[pltpu.VMEM((2, HALO + tm, ML_PROJ_CHUNK), F32),
                        pltpu.VMEM((qk_w // 2, tm), BF16),
                        pltpu.VMEM((tm, qk_w // 2), BF16),
                        pltpu.VMEM((ML_HEADS * ML_VT_ROWS, tm), BF16),
                        pltpu.VMEM((tm, v_w), BF16),
                        pltpu.VMEM((tm, ML_GATE_PAD), F32),
                        pltpu.VMEM((2 * ML_HEADS, tm), F32),
                        pltpu.VMEM((ML_HEADS, ML_VT_ROWS, ML_QK_DIM), F32),
                        pltpu.VMEM((ML_HEADS, 1, 1), F32)],
        compiler_params=_params("arbitrary"),
        name="mlstm_layer",
    )(x2, x2, xnorm_g.reshape(1, d), w_all, conv_w, conv_b.reshape(1, qk_w),
      gate_b.reshape(1, ML_GATE_PAD), head_norm.reshape(1, v_w))


def kernel(x, rel_bias, attn_norm, attn_w_in, attn_lambda_q1, attn_lambda_k1, attn_lambda_q2,
           attn_lambda_k2, attn_subln, attn_w_out, mlstm_norm, mlstm_w_in, mlstm_b_i, mlstm_b_f,
           mlstm_conv_w, mlstm_conv_b, mlstm_head_norm, mlstm_w_out, mlp_norm, mlp_w1, mlp_w2,
           final_norm):
    batch, seq_len, d = x.shape
    t = batch * seq_len
    h = x.reshape(t, d)

    lambda_init = 0.8 - 0.6 * math.exp(-0.3 * 0)
    q_t, k, v_t = _attn_proj(h, attn_norm[0], attn_w_in[0].astype(BF16), batch, seq_len)
    band = _bias_band(rel_bias, ATTN_TILE)
    lam_params = jnp.stack([attn_lambda_q1[0], attn_lambda_k1[0],
                            attn_lambda_q2[0], attn_lambda_k2[0]]).astype(F32)
    attn = _diff_attn(q_t, k, v_t, band, lam_params, attn_subln[0], lambda_init)
    h = _out_mlp(attn.reshape(t, -1), h, attn_w_out[0].astype(BF16), mlp_norm[0],
                 mlp_w1[0].astype(BF16), mlp_w2[0].astype(BF16), final_norm, final_norm=False)

    n_gates = 2 * ML_HEADS
    w_all = jnp.pad(mlstm_w_in[0], ((0, 0), (0, ML_GATE_PAD - n_gates))).astype(BF16)
    gate_b = jnp.pad(jnp.concatenate([mlstm_b_i[0], mlstm_b_f[0]]).astype(F32),
                     (0, ML_GATE_PAD - n_gates))
    mix = _mlstm_layer(h, mlstm_norm[0], w_all, mlstm_conv_w[0], mlstm_conv_b[0], gate_b,
                       mlstm_head_norm[0], batch, seq_len)
    out = _out_mlp(mix, h, mlstm_w_out[0].astype(BF16), mlp_norm[1],
                   mlp_w1[1].astype(BF16), mlp_w2[1].astype(BF16), final_norm, final_norm=True)
    return out.reshape(batch, seq_len, d)
```

```python
import functools
import math

import numpy as np
import jax
import jax.numpy as jnp
from jax import lax
from jax.experimental import pallas as pl
from jax.experimental.pallas import tpu as pltpu

F32 = jnp.float32
BF16 = jnp.bfloat16

EPS = 1e-6
NEG = -1e30

LANES = 128
F32_SUBLANES = 8
BF16_SUBLANES = 16
VMEM_LIMIT = 56 * 1024 * 1024

DA_HEADS = 8
DA_HEAD_DIM = 64
DA_V_DIM = 2 * DA_HEAD_DIM
REL_BUCKETS = 32
REL_MAX_DIST = 128
DA_VT_ROWS = DA_V_DIM + BF16_SUBLANES
LOG2E = math.log2(math.e)

ML_HEADS = 4
ML_QK_DIM = 128
ML_V_DIM = 256
CONV_WIDTH = 4
ML_GATE_PAD = LANES
ML_VT_ROWS = ML_V_DIM + BF16_SUBLANES

ROW_TILE = 512
ATTN_PROJ_ROW_TILE = 1024
ATTN_PROJ_CHUNK = 512
MLP_FF_CHUNK = 1024
MLP_ROW_TILE = 1024
ATTN_TILE = 256
ML_CHUNK = 256
ML_PROJ_CHUNK = 256
ML_PROJ_AHEAD = 2
ATTN_PIPE_DEPTH = 4
ATTN_RING = 4
ATTN_TILES_PER_ITER = 2
HALO = BF16_SUBLANES


def _params(*sem):
    return pltpu.CompilerParams(dimension_semantics=sem, vmem_limit_bytes=VMEM_LIMIT)


def _const_spec(shape):
    nd = len(shape)
    return pl.BlockSpec(shape, lambda *_: (0,) * nd)


def _rmsnorm_rows(x, g):
    return x * lax.rsqrt(jnp.mean(x * x, axis=-1, keepdims=True) + EPS) * g


def _attn_proj_kernel(x_ref, g_ref, w_ref, qt_ref, k_ref, vt_ref, *, n_chunk, width, q_scale):
    xn = _rmsnorm_rows(x_ref[...], g_ref[...]).astype(BF16)
    tm = x_ref.shape[0]
    per_part = width // n_chunk
    heads_per_chunk = n_chunk // DA_V_DIM
    ones = jnp.ones((DA_VT_ROWS - DA_V_DIM, tm), BF16)
    for c in range(3 * per_part):
        y = jnp.dot(xn, w_ref[:, c * n_chunk:(c + 1) * n_chunk], preferred_element_type=F32)
        part, pc = divmod(c, per_part)
        if part == 0:
            qt_ref[0, pc * n_chunk:(pc + 1) * n_chunk, :] = jnp.transpose(y * q_scale).astype(BF16)
        elif part == 1:
            k_ref[:, pc * n_chunk:(pc + 1) * n_chunk] = y.astype(BF16)
        else:
            y_t = jnp.transpose(y).astype(BF16)
            for hh in range(heads_per_chunk):
                r0 = (pc * heads_per_chunk + hh) * DA_VT_ROWS
                vt_ref[0, r0:r0 + DA_V_DIM, :] = y_t[hh * DA_V_DIM:(hh + 1) * DA_V_DIM]
                vt_ref[0, r0 + DA_V_DIM:r0 + DA_VT_ROWS, :] = ones


def _attn_proj(x2, g, w_bf16, batch, seq_len):
    t, d = x2.shape
    width = w_bf16.shape[1] // 3
    n_chunk = ATTN_PROJ_CHUNK
    tm = ATTN_PROJ_ROW_TILE
    assert seq_len % tm == 0 and width % n_chunk == 0 and width == DA_HEADS * DA_V_DIM
    tiles = seq_len // tm
    kern = functools.partial(_attn_proj_kernel, n_chunk=n_chunk, width=width,
                             q_scale=DA_HEAD_DIM ** -0.5 * LOG2E)
    return pl.pallas_call(
        kern,
        grid=(t // tm,),
        in_specs=[pl.BlockSpec((tm, d), lambda i: (i, 0)),
                  _const_spec((1, d)),
                  _const_spec((d, 3 * width))],
        out_specs=(pl.BlockSpec((1, width, tm), lambda i: (i // tiles, 0, i % tiles)),
                   pl.BlockSpec((tm, width), lambda i: (i, 0)),
                   pl.BlockSpec((1, DA_HEADS * DA_VT_ROWS, tm), lambda i: (i // tiles, 0, i % tiles))),
        out_shape=(jax.ShapeDtypeStruct((batch, width, seq_len), BF16),
                   jax.ShapeDtypeStruct((t, width), BF16),
                   jax.ShapeDtypeStruct((batch, DA_HEADS * DA_VT_ROWS, seq_len), BF16)),
        compiler_params=_params("parallel"),
        name="attn_proj",
    )(x2, g.reshape(1, d), w_bf16)


def _bucket_band(tile):
    r = np.arange(3 * tile)[:, None]
    i = np.arange(tile)[None, :]
    dist = i + 2 * tile - r
    max_exact = REL_BUCKETS // 2
    d = np.maximum(dist, 1).astype(np.float32)
    large = max_exact + (np.log(d / np.float32(max_exact)) / np.float32(math.log(REL_MAX_DIST / max_exact))
                         * np.float32(REL_BUCKETS - max_exact)).astype(np.int32)
    large = np.minimum(large, REL_BUCKETS - 1)
    bucket = np.where(dist < max_exact, dist, large)
    bucket = np.where(dist < 0, -1, bucket).astype(np.int32)
    assert (bucket[:tile] == REL_BUCKETS - 1).all()
    return bucket


def _bias_band_kernel(rel_ref, bucket_ref, o_ref):
    h = pl.program_id(0)
    bucket = bucket_ref[...]
    acc = jnp.full(bucket.shape, NEG, F32)
    for b in range(REL_BUCKETS):
        acc = jnp.where(bucket == b, rel_ref[b, h] * LOG2E, acc)
    o_ref[0] = acc


def _bias_band(rel_bias, tile):
    bucket = jnp.asarray(_bucket_band(tile))
    return pl.pallas_call(
        _bias_band_kernel,
        grid=(DA_HEADS,),
        in_specs=[pl.BlockSpec(memory_space=pltpu.SMEM),
                  _const_spec((3 * tile, tile))],
        out_specs=pl.BlockSpec((1, 3 * tile, tile), lambda h: (h, 0, 0)),
        out_shape=jax.ShapeDtypeStruct((DA_HEADS, 3 * tile, tile), F32),
        compiler_params=_params("arbitrary"),
        name="bias_band",
    )(rel_bias, bucket)


def _diff_attn_kernel(lam_ref, qt_ref, k_ref, vt_ref, band_ref, sub_ref, o_ref,
                      qz_ref, m_ref, acc_ref, sb_ref, cm_ref, *, tile, n_tiles, lambda_init):
    head_cols = [slice(h * DA_V_DIM, (h + 1) * DA_V_DIM) for h in range(DA_HEADS)]
    depth = ATTN_PIPE_DEPTH
    slots = sb_ref.shape[0]
    per_iter = ATTN_TILES_PER_ITER

    def tile_span(j):
        return pl.ds(pl.multiple_of(j * tile, tile), tile)

    def scores(j, h):
        return jnp.dot(k_ref[tile_span(j), head_cols[h]], qz_ref[h], preferred_element_type=F32)

    def drain(i, j, h, s_raw):
        kind = jnp.clip(j - (i - 2), 0, 2)
        bias = band_ref[h, pl.ds(pl.multiple_of(kind * tile, tile), tile), :]
        sb = s_raw + jnp.concatenate([bias, bias], axis=1)
        sb_ref[h % slots] = sb
        cm_ref[h % slots] = jnp.max(sb.reshape(tile // F32_SUBLANES, F32_SUBLANES, 2 * tile), axis=0)

    def consume(j, h):
        m = m_ref[h]
        m_new = jnp.maximum(m, jnp.max(cm_ref[h % slots], axis=0, keepdims=True))
        m_ref[h] = m_new
        p = jnp.exp2((sb_ref[h % slots] - m_new).astype(BF16))
        vt_blk = vt_ref[0, h * DA_VT_ROWS:(h + 1) * DA_VT_ROWS, tile_span(j)]
        return jnp.dot(vt_blk, p, preferred_element_type=F32), jnp.exp2(m - m_new)

    def rescale_add(h, pv, alpha):
        acc_ref[h] = alpha * acc_ref[h] + pv

    def item(j, n):
        return jnp.minimum(j + n // DA_HEADS, n_tiles - 1), n % DA_HEADS

    def run_tiles(i, j0, tiles):
        issued = None
        pending = None
        for n in range(tiles * DA_HEADS):
            jn, hn = item(j0, n + depth)
            nxt = (i, jn, hn, scores(jn, hn))
            pv, alpha = consume(j0 + n // DA_HEADS, n % DA_HEADS)
            if issued is not None:
                drain(*issued)
            if pending is not None:
                rescale_add(*pending)
            issued, pending = nxt, (n % DA_HEADS, pv, alpha)
        drain(*issued)
        rescale_add(*pending)

    def open_query_tile(i):
        sub = lax.broadcasted_iota(jnp.int32, (DA_V_DIM, tile), 0)
        first_map = (sub < DA_HEAD_DIM).astype(F32)
        for h in range(DA_HEADS):
            q_t = qt_ref[0, head_cols[h], tile_span(i)].astype(F32)
            qz_ref[h] = jnp.concatenate([q_t * first_map, q_t * (1.0 - first_map)], axis=1).astype(BF16)
        for n in range(depth):
            drain(i, 0, n, scores(0, n))

    lp = lam_ref[...]
    lam = (jnp.exp(jnp.sum(lp[0:1] * lp[1:2], axis=-1, keepdims=True))
           - jnp.exp(jnp.sum(lp[2:3] * lp[3:4], axis=-1, keepdims=True)) + lambda_init)
    sub_w = sub_ref[...] * (1.0 - lambda_init)

    def query_tile(i, carry):
        m_ref[...] = jnp.full(m_ref.shape, NEG, F32)
        acc_ref[...] = jnp.zeros(acc_ref.shape, F32)

        n_key_tiles = i + 1

        def multi_tile_body(jj, c):
            run_tiles(i, jj * per_iter, per_iter)
            return c

        lax.fori_loop(0, n_key_tiles // per_iter, multi_tile_body, 0)
        for r in range(1, per_iter):
            @pl.when(n_key_tiles % per_iter == r)
            def _(r=r):
                run_tiles(i, n_key_tiles - r, r)

        for h in range(DA_HEADS):
            inv_l = 1.0 / acc_ref[h, DA_V_DIM:DA_V_DIM + 1, :]
            o = (acc_ref[h, 0:DA_V_DIM, :tile] * inv_l[:, :tile]
                 - acc_ref[h, 0:DA_V_DIM, tile:] * (lam * inv_l[:, tile:]))
            o = o * lax.rsqrt(jnp.mean(o * o, axis=0, keepdims=True) + EPS)
            o_ref[0, tile_span(i), head_cols[h]] = (jnp.transpose(o) * sub_w).astype(o_ref.dtype)
        open_query_tile(jnp.minimum(i + 1, n_tiles - 1))
        return carry

    open_query_tile(0)
    lax.fori_loop(0, n_tiles, query_tile, 0)


def _diff_attn(q_t, k, v_t, band, lam_params, subln, lambda_init):
    b, width, s = q_t.shape
    tile = ATTN_TILE
    assert s % tile == 0 and width == DA_HEADS * DA_V_DIM
    kern = functools.partial(_diff_attn_kernel, tile=tile, n_tiles=s // tile, lambda_init=lambda_init)
    return pl.pallas_call(
        kern,
        grid=(b,),
        in_specs=[_const_spec(lam_params.shape),
                  pl.BlockSpec((1, width, s), lambda bi: (bi, 0, 0)),
                  pl.BlockSpec((s, width), lambda bi: (bi, 0)),
                  pl.BlockSpec((1, DA_HEADS * DA_VT_ROWS, s), lambda bi: (bi, 0, 0)),
                  pl.BlockSpec(band.shape, lambda bi: (0, 0, 0), pipeline_mode=pl.Buffered(1)),
                  _const_spec((1, DA_V_DIM))],
        out_specs=pl.BlockSpec((1, s, width), lambda bi: (bi, 0, 0)),
        out_shape=jax.ShapeDtypeStruct((b, s, width), BF16),
        scratch_shapes=[pltpu.VMEM((DA_HEADS, DA_V_DIM, 2 * tile), BF16),
                        pltpu.VMEM((DA_HEADS, 1, 2 * tile), F32),
                        pltpu.VMEM((DA_HEADS, DA_VT_ROWS, 2 * tile), F32),
                        pltpu.VMEM((ATTN_RING, tile, 2 * tile), F32),
                        pltpu.VMEM((ATTN_RING, F32_SUBLANES, 2 * tile), F32)],
        compiler_params=_params("parallel"),
        name="diff_attn",
    )(lam_params, q_t, k, v_t, band, subln.reshape(1, DA_V_DIM))


def _out_mlp_kernel(a_ref, h_ref, wo_ref, g_ref, w1_ref, w2_ref, gf_ref, o_ref, u_ref,
                    *, ff_chunk, final_norm):
    h1 = h_ref[...] + jnp.dot(a_ref[...], wo_ref[...], preferred_element_type=F32)
    xn = _rmsnorm_rows(h1, g_ref[...]).astype(BF16)
    d_ff = w1_ref.shape[1]
    for c in range(d_ff // ff_chunk):
        sl = slice(c * ff_chunk, (c + 1) * ff_chunk)
        u = jnp.maximum(jnp.dot(xn, w1_ref[:, sl], preferred_element_type=F32), 0.0)
        u_ref[:, sl] = (u * u).astype(BF16)
    h2 = h1 + jnp.dot(u_ref[...], w2_ref[...], preferred_element_type=F32)
    if final_norm:
        h2 = _rmsnorm_rows(h2, gf_ref[...])
    o_ref[...] = h2


def _out_mlp(a2, h2, wo, g, w1, w2, gf, final_norm):
    t, d = h2.shape
    d_ff = w1.shape[1]
    kern = functools.partial(_out_mlp_kernel, ff_chunk=MLP_FF_CHUNK, final_norm=final_norm)
    single = pl.Buffered(1)
    return pl.pallas_call(
        kern,
        grid=(t // MLP_ROW_TILE,),
        in_specs=[pl.BlockSpec((MLP_ROW_TILE, a2.shape[1]), lambda i: (i, 0)),
                  pl.BlockSpec((MLP_ROW_TILE, d), lambda i: (i, 0)),
                  pl.BlockSpec(wo.shape, lambda i: (0, 0), pipeline_mode=single),
                  _const_spec((1, d)),
                  pl.BlockSpec(w1.shape, lambda i: (0, 0), pipeline_mode=single),
                  pl.BlockSpec(w2.shape, lambda i: (0, 0), pipeline_mode=single),
                  _const_spec((1, d))],
        out_specs=pl.BlockSpec((MLP_ROW_TILE, d), lambda i: (i, 0)),
        out_shape=jax.ShapeDtypeStruct((t, d), F32),
        scratch_shapes=[pltpu.VMEM((MLP_ROW_TILE, d_ff), BF16)],
        compiler_params=_params("parallel"),
        name="out_mlp_final" if final_norm else "out_mlp",
    )(a2, h2, wo, g.reshape(1, d), w1, w2, gf.reshape(1, d))


def _log_sigmoid(x):
    return -(jnp.maximum(-x, 0.0) + jnp.log(1.0 + jnp.exp(-jnp.abs(x))))


def _mlstm_layer_kernel(x_ref, xh_ref, g_ref, w_ref, cw_ref, cb_ref, gb_ref, hn_ref, o_ref,
                        cs_ref, qt_ref, k_ref, vt_ref, og_ref, gcol_ref, grow_ref, s_ref, m_ref,
                        *, tiles_per_seq, q_scale, qk_w, v_w):
    i = pl.program_id(0)
    tm = x_ref.shape[0]
    nc = ML_PROJ_CHUNK
    L = ML_CHUNK
    seq_start = (i % tiles_per_seq) == 0

    @pl.when(seq_start)
    def _():
        s_ref[...] = jnp.zeros_like(s_ref)
        m_ref[...] = jnp.zeros_like(m_ref)

    g = g_ref[...]
    xn = _rmsnorm_rows(x_ref[...], g).astype(BF16)
    xh = _rmsnorm_rows(xh_ref[...], g).astype(BF16)
    ones = jnp.ones((ML_VT_ROWS - ML_V_DIM, tm), BF16)
    n_chunks = (qk_w + 2 * v_w) // nc
    gate_chunk = n_chunks

    def proj_matmul(c):
        if c == gate_chunk:
            return jnp.dot(xn, w_ref[:, c * nc:c * nc + ML_GATE_PAD], preferred_element_type=F32)
        return jnp.dot(xn, w_ref[:, c * nc:(c + 1) * nc], preferred_element_type=F32)

    def proj_epilogue(c, y):
        col0 = c * nc
        cols = slice(col0, col0 + nc)
        if c == gate_chunk:
            gates = y + gb_ref[...]
            lane = lax.broadcasted_iota(jnp.int32, gates.shape, 1)
            gates = jnp.where(lane < ML_HEADS, gates, _log_sigmoid(gates))
            gcol_ref[...] = gates
            grow_ref[...] = jnp.transpose(gates)[0:grow_ref.shape[0], :]
        elif col0 < qk_w:
            slot = c % 2
            halo = jnp.dot(xh, w_ref[:, cols], preferred_element_type=F32)
            cs_ref[slot, 0:HALO, :] = jnp.where(seq_start, 0.0, halo)
            cs_ref[slot, HALO:HALO + tm, :] = y
            conv = cb_ref[:, cols]
            for j in range(CONV_WIDTH):
                off = HALO - (CONV_WIDTH - 1) + j
                conv = conv + cw_ref[j:j + 1, cols] * cs_ref[slot, off:off + tm, :]
            act = conv * jax.nn.sigmoid(conv)
            if col0 < qk_w // 2:
                qt_ref[cols, :] = jnp.transpose(act * q_scale).astype(BF16)
            else:
                k_ref[:, col0 - qk_w // 2:col0 - qk_w // 2 + nc] = act.astype(BF16)
        elif col0 < qk_w + v_w:
            r0 = ((col0 - qk_w) // ML_V_DIM) * ML_VT_ROWS
            vt_ref[r0:r0 + ML_V_DIM, :] = jnp.transpose(y).astype(BF16)
            vt_ref[r0 + ML_V_DIM:r0 + ML_VT_ROWS, :] = ones
        else:
            o0 = col0 - qk_w - v_w
            og_ref[:, o0:o0 + nc] = jax.nn.sigmoid(y).astype(BF16)

    row = lax.broadcasted_iota(jnp.int32, (L, L), 0)
    col = lax.broadcasted_iota(jnp.int32, (L, L), 1)
    key_le_query = row <= col

    def qk_blocks(c, h):
        pos = slice(c * L, (c + 1) * L)
        return (qt_ref[h * ML_QK_DIM:(h + 1) * ML_QK_DIM, pos],
                k_ref[pos, h * ML_QK_DIM:(h + 1) * ML_QK_DIM])

    def rec_matmuls(c, h):
        q_t, kh = qk_blocks(c, h)
        s_t = jnp.dot(kh, q_t, preferred_element_type=F32)
        sq = jnp.dot(s_ref[h].astype(BF16), q_t, preferred_element_type=F32)
        return s_t, sq

    def rec_head(c, h, s_t, sq):
        pos = slice(c * L, (c + 1) * L)
        _, kh = qk_blocks(c, h)
        vcols = slice(h * ML_V_DIM, (h + 1) * ML_V_DIM)
        v_t = vt_ref[h * ML_VT_ROWS:(h + 1) * ML_VT_ROWS, pos]
        ig_r = grow_ref[h:h + 1, pos]
        lf_r = grow_ref[ML_HEADS + h:ML_HEADS + h + 1, pos]
        ig_c = gcol_ref[pos, h:h + 1]
        lf_c = gcol_ref[pos, ML_HEADS + h:ML_HEADS + h + 1]
        m_prev = m_ref[h]

        b_r = jnp.sum(jnp.where(key_le_query, lf_c, 0.0), axis=0, keepdims=True)
        b_c = jnp.sum(jnp.where(key_le_query, 0.0, lf_r), axis=1, keepdims=True) + lf_c
        b_last = jnp.sum(lf_r, axis=1, keepdims=True)

        d_t = jnp.where(key_le_query, b_r - (b_c - ig_c), NEG)
        m_inter = b_r + m_prev
        m_j = jnp.maximum(m_inter, jnp.max(d_t, axis=0, keepdims=True))
        sc_t = s_t * jnp.exp(d_t - m_j)
        inter = jnp.exp(m_inter - m_j)
        tot = jnp.dot(v_t, sc_t.astype(BF16), preferred_element_type=F32) + inter * sq
        den = tot[ML_V_DIM:ML_V_DIM + 1, :]
        h_t = tot[0:ML_V_DIM, :] * (1.0 / jnp.maximum(jnp.abs(den), jnp.exp(-m_j)))
        y_t = h_t * lax.rsqrt(jnp.mean(h_t * h_t, axis=0, keepdims=True) + EPS)
        y = jnp.transpose(y_t) * hn_ref[:, vcols]
        o_ref[pos, vcols] = (og_ref[pos, vcols].astype(F32) * y).astype(o_ref.dtype)

        a_r = b_last - b_r + ig_r
        m_new = jnp.maximum(b_last + m_prev, jnp.max(a_r, axis=1, keepdims=True))
        w_r = jnp.exp(a_r - m_new)
        decay = jnp.exp(b_last + m_prev - m_new)
        wv_t = (v_t.astype(F32) * w_r).astype(BF16)
        s_ref[h] = decay * s_ref[h] + jnp.dot(wv_t, kh, preferred_element_type=F32)
        m_ref[h] = m_new

    qk_chunks = list(range(qk_w // nc))
    v_chunks = list(range(qk_w // nc, (qk_w + v_w) // nc))
    o_chunks = list(range((qk_w + v_w) // nc, n_chunks))
    first = [gate_chunk]
    while qk_chunks or v_chunks:
        first += v_chunks[:1] + qk_chunks[:1]
        qk_chunks, v_chunks = qk_chunks[1:], v_chunks[1:]
    rec_items = [(c, h) for c in range(tm // L) for h in range(ML_HEADS)]
    assert len(o_chunks) == ML_HEADS and ML_PROJ_AHEAD < ML_HEADS
    stages = [("proj", c) for c in first]
    for h in range(ML_HEADS):
        stages += [("proj", o_chunks[h]), ("rec", rec_items[h])]
    stages += [("rec", it) for it in rec_items[ML_HEADS:]]

    def issue(stage):
        kind, arg = stage
        return proj_matmul(arg) if kind == "proj" else rec_matmuls(*arg)

    ahead = ML_PROJ_AHEAD
    n_proj = len(first)
    issued = {}
    for n in range(min(ahead, n_proj)):
        issued[n] = issue(stages[n])
    for n, stage in enumerate(stages):
        nxt = n + ahead
        if nxt < len(stages) and nxt not in issued:
            if stages[nxt][0] == "proj" or n >= n_proj - 1:
                issued[nxt] = issue(stages[nxt])
        if n not in issued:
            issued[n] = issue(stage)
        kind, arg = stage
        if kind == "proj":
            proj_epilogue(arg, issued.pop(n))
        else:
            rec_head(*arg, *issued.pop(n))


def _mlstm_layer(x2, xnorm_g, w_all, conv_w, conv_b, gate_b, head_norm, batch, seq_len):
    t, d = x2.shape
    tm = ROW_TILE
    qk_w = 2 * ML_HEADS * ML_QK_DIM
    v_w = ML_HEADS * ML_V_DIM
    assert seq_len % tm == 0 and tm % HALO == 0 and tm % ML_CHUNK == 0 and ML_PROJ_CHUNK == ML_V_DIM
    assert w_all.shape[1] == qk_w + 2 * v_w + ML_GATE_PAD
    tiles = seq_len // tm
    kern = functools.partial(_mlstm_layer_kernel, tiles_per_seq=tiles, q_scale=ML_QK_DIM ** -0.5,
                             qk_w=qk_w, v_w=v_w)
    halo_blocks = tm // HALO
    return pl.pallas_call(
        kern,
        grid=(t // tm,),
        in_specs=[pl.BlockSpec((tm, d), lambda i: (i, 0)),
                  pl.BlockSpec((HALO, d), lambda i: (jnp.maximum(i * halo_blocks - 1, 0), 0)),
                  _const_spec((1, d)),
                  pl.BlockSpec(w_all.shape, lambda i: (0, 0), pipeline_mode=pl.Buffered(1)),
                  _const_spec(conv_w.shape),
                  _const_spec((1, qk_w)),
                  _const_spec((1, ML_GATE_PAD)),
                  _const_spec((1, v_w))],
        out_specs=pl.BlockSpec((tm, v_w), lambda i: (i, 0)),
        out_shape=jax.ShapeDtypeStruct((t, v_w), BF16),
        scratch_shapes=[pltpu.VMEM((2, HALO + tm, ML_PROJ_CHUNK), F32),
                        pltpu.VMEM((qk_w // 2, tm), BF16),
                        pltpu.VMEM((tm, qk_w // 2), BF16),
                        pltpu.VMEM((ML_HEADS * ML_VT_ROWS, tm), BF16),
                        pltpu.VMEM((tm, v_w), BF16),
                        pltpu.VMEM((tm, ML_GATE_PAD), F32),
                        pltpu.VMEM((2 * ML_HEADS, tm), F32),
                        pltpu.VMEM((ML_HEADS, ML_VT_ROWS, ML_QK_DIM), F32),
                        pltpu.VMEM((ML_HEADS, 1, 1), F32)],
        compiler_params=_params("arbitrary"),
        name="mlstm_layer",
    )(x2, x2, xnorm_g.reshape(1, d), w_all, conv_w, conv_b.reshape(1, qk_w),
      gate_b.reshape(1, ML_GATE_PAD), head_norm.reshape(1, v_w))


def kernel(x, rel_bias, attn_norm, attn_w_in, attn_lambda_q1, attn_lambda_k1, attn_lambda_q2,
           attn_lambda_k2, attn_subln, attn_w_out, mlstm_norm, mlstm_w_in, mlstm_b_i, mlstm_b_f,
           mlstm_conv_w, mlstm_conv_b, mlstm_head_norm, mlstm_w_out, mlp_norm, mlp_w1, mlp_w2,
           final_norm):
    batch, seq_len, d = x.shape
    t = batch * seq_len
    h = x.reshape(t, d)

    lambda_init = 0.8 - 0.6 * math.exp(-0.3 * 0)
    q_t, k, v_t = _attn_proj(h, attn_norm[0], attn_w_in[0].astype(BF16), batch, seq_len)
    band = _bias_band(rel_bias, ATTN_TILE)
    lam_params = jnp.stack([attn_lambda_q1[0], attn_lambda_k1[0],
                            attn_lambda_q2[0], attn_lambda_k2[0]]).astype(F32)
    attn = _diff_attn(q_t, k, v_t, band, lam_params, attn_subln[0], lambda_init)
    h = _out_mlp(attn.reshape(t, -1), h, attn_w_out[0].astype(BF16), mlp_norm[0],
                 mlp_w1[0].astype(BF16), mlp_w2[0].astype(BF16), final_norm, final_norm=False)

    n_gates = 2 * ML_HEADS
    w_all = jnp.pad(mlstm_w_in[0], ((0, 0), (0, ML_GATE_PAD - n_gates))).astype(BF16)
    gate_b = jnp.pad(jnp.concatenate([mlstm_b_i[0], mlstm_b_f[0]]).astype(F32),
                     (0, ML_GATE_PAD - n_gates))
    mix = _mlstm_layer(h, mlstm_norm[0], w_all, mlstm_conv_w[0], mlstm_conv_b[0], gate_b,
                       mlstm_head_norm[0], batch, seq_len)
    out = _out_mlp(mix, h, mlstm_w_out[0].astype(BF16), mlp_norm[1],
                   mlp_w1[1].astype(BF16), mlp_w2[1].astype(BF16), final_norm, final_norm=True)
    return out.reshape(batch, seq_len, d)
```

```python
import functools
import math

import numpy as np
import jax
import jax.numpy as jnp
from jax import lax
from jax.experimental import pallas as pl
from jax.experimental.pallas import tpu as pltpu

F32 = jnp.float32
BF16 = jnp.bfloat16

EPS = 1e-6
NEG = -1e30

LANES = 128
F32_SUBLANES = 8
BF16_SUBLANES = 16
VMEM_LIMIT = 56 * 1024 * 1024

DA_HEADS = 8
DA_HEAD_DIM = 64
DA_V_DIM = 2 * DA_HEAD_DIM
REL_BUCKETS = 32
REL_MAX_DIST = 128
DA_VT_ROWS = DA_V_DIM + BF16_SUBLANES
LOG2E = math.log2(math.e)

ML_HEADS = 4
ML_QK_DIM = 128
ML_V_DIM = 256
CONV_WIDTH = 4
ML_GATE_PAD = LANES
ML_VT_ROWS = ML_V_DIM + BF16_SUBLANES

ROW_TILE = 1024
ATTN_PROJ_ROW_TILE = 1024
ATTN_PROJ_CHUNK = 512
MLP_FF_CHUNK = 1024
MLP_ROW_TILE = 1024
ATTN_TILE = 256
ML_CHUNK = 256
ML_PROJ_CHUNK = 256
ML_PROJ_AHEAD = 2
ATTN_PIPE_DEPTH = 4
ATTN_RING = 4
ATTN_TILES_PER_ITER = 2
HALO = BF16_SUBLANES


def _params(*sem):
    return pltpu.CompilerParams(dimension_semantics=sem, vmem_limit_bytes=VMEM_LIMIT)


def _const_spec(shape):
    nd = len(shape)
    return pl.BlockSpec(shape, lambda *_: (0,) * nd)


def _rmsnorm_rows(x, g):
    return x * lax.rsqrt(jnp.mean(x * x, axis=-1, keepdims=True) + EPS) * g


def _attn_proj_kernel(x_ref, g_ref, w_ref, qt_ref, k_ref, vt_ref, *, n_chunk, width, q_scale):
    xn = _rmsnorm_rows(x_ref[...], g_ref[...]).astype(BF16)
    tm = x_ref.shape[0]
    per_part = width // n_chunk
    heads_per_chunk = n_chunk // DA_V_DIM
    ones = jnp.ones((DA_VT_ROWS - DA_V_DIM, tm), BF16)
    for c in range(3 * per_part):
        y = jnp.dot(xn, w_ref[:, c * n_chunk:(c + 1) * n_chunk], preferred_element_type=F32)
        part, pc = divmod(c, per_part)
        if part == 0:
            qt_ref[0, pc * n_chunk:(pc + 1) * n_chunk, :] = jnp.transpose(y * q_scale).astype(BF16)
        elif part == 1:
            k_ref[:, pc * n_chunk:(pc + 1) * n_chunk] = y.astype(BF16)
        else:
            y_t = jnp.transpose(y).astype(BF16)
            for hh in range(heads_per_chunk):
                r0 = (pc * heads_per_chunk + hh) * DA_VT_ROWS
                vt_ref[0, r0:r0 + DA_V_DIM, :] = y_t[hh * DA_V_DIM:(hh + 1) * DA_V_DIM]
                vt_ref[0, r0 + DA_V_DIM:r0 + DA_VT_ROWS, :] = ones


def _attn_proj(x2, g, w_bf16, batch, seq_len):
    t, d = x2.shape
    width = w_bf16.shape[1] // 3
    n_chunk = ATTN_PROJ_CHUNK
    tm = ATTN_PROJ_ROW_TILE
    assert seq_len % tm == 0 and width % n_chunk == 0 and width == DA_HEADS * DA_V_DIM
    tiles = seq_len // tm
    kern = functools.partial(_attn_proj_kernel, n_chunk=n_chunk, width=width,
                             q_scale=DA_HEAD_DIM ** -0.5 * LOG2E)
    return pl.pallas_call(
        kern,
        grid=(t // tm,),
        in_specs=[pl.BlockSpec((tm, d), lambda i: (i, 0)),
                  _const_spec((1, d)),
                  _const_spec((d, 3 * width))],
        out_specs=(pl.BlockSpec((1, width, tm), lambda i: (i // tiles, 0, i % tiles)),
                   pl.BlockSpec((tm, width), lambda i: (i, 0)),
                   pl.BlockSpec((1, DA_HEADS * DA_VT_ROWS, tm), lambda i: (i // tiles, 0, i % tiles))),
        out_shape=(jax.ShapeDtypeStruct((batch, width, seq_len), BF16),
                   jax.ShapeDtypeStruct((t, width), BF16),
                   jax.ShapeDtypeStruct((batch, DA_HEADS * DA_VT_ROWS, seq_len), BF16)),
        compiler_params=_params("parallel"),
        name="attn_proj",
    )(x2, g.reshape(1, d), w_bf16)


def _bucket_band(tile):
    r = np.arange(3 * tile)[:, None]
    i = np.arange(tile)[None, :]
    dist = i + 2 * tile - r
    max_exact = REL_BUCKETS // 2
    d = np.maximum(dist, 1).astype(np.float32)
    large = max_exact + (np.log(d / np.float32(max_exact)) / np.float32(math.log(REL_MAX_DIST / max_exact))
                         * np.float32(REL_BUCKETS - max_exact)).astype(np.int32)
    large = np.minimum(large, REL_BUCKETS - 1)
    bucket = np.where(dist < max_exact, dist, large)
    bucket = np.where(dist < 0, -1, bucket).astype(np.int32)
    assert (bucket[:tile] == REL_BUCKETS - 1).all()
    return bucket


def _bias_band_kernel(rel_ref, bucket_ref, o_ref):
    h = pl.program_id(0)
    bucket = bucket_ref[...]
    acc = jnp.full(bucket.shape, NEG, F32)
    for b in range(REL_BUCKETS):
        acc = jnp.where(bucket == b, rel_ref[b, h] * LOG2E, acc)
    o_ref[0] = acc


def _bias_band(rel_bias, tile):
    bucket = jnp.asarray(_bucket_band(tile))
    return pl.pallas_call(
        _bias_band_kernel,
        grid=(DA_HEADS,),
        in_specs=[pl.BlockSpec(memory_space=pltpu.SMEM),
                  _const_spec((3 * tile, tile))],
        out_specs=pl.BlockSpec((1, 3 * tile, tile), lambda h: (h, 0, 0)),
        out_shape=jax.ShapeDtypeStruct((DA_HEADS, 3 * tile, tile), F32),
        compiler_params=_params("arbitrary"),
        name="bias_band",
    )(rel_bias, bucket)


def _diff_attn_kernel(lam_ref, qt_ref, k_ref, vt_ref, band_ref, sub_ref, o_ref,
                      qz_ref, m_ref, acc_ref, sb_ref, cm_ref, *, tile, n_tiles, lambda_init):
    head_cols = [slice(h * DA_V_DIM, (h + 1) * DA_V_DIM) for h in range(DA_HEADS)]
    depth = ATTN_PIPE_DEPTH
    slots = sb_ref.shape[0]
    per_iter = ATTN_TILES_PER_ITER

    def tile_span(j):
        return pl.ds(pl.multiple_of(j * tile, tile), tile)

    def scores(j, h):
        return jnp.dot(k_ref[tile_span(j), head_cols[h]], qz_ref[h], preferred_element_type=F32)

    def drain(i, j, h, s_raw):
        kind = jnp.clip(j - (i - 2), 0, 2)
        bias = band_ref[h, pl.ds(pl.multiple_of(kind * tile, tile), tile), :]
        sb = s_raw + jnp.concatenate([bias, bias], axis=1)
        sb_ref[h % slots] = sb
        cm_ref[h % slots] = jnp.max(sb.reshape(tile // F32_SUBLANES, F32_SUBLANES, 2 * tile), axis=0)

    def consume(j, h):
        m = m_ref[h]
        m_new = jnp.maximum(m, jnp.max(cm_ref[h % slots], axis=0, keepdims=True))
        m_ref[h] = m_new
        p = jnp.exp2((sb_ref[h % slots] - m_new).astype(BF16))
        vt_blk = vt_ref[0, h * DA_VT_ROWS:(h + 1) * DA_VT_ROWS, tile_span(j)]
        return jnp.dot(vt_blk, p, preferred_element_type=F32), jnp.exp2(m - m_new)

    def rescale_add(h, pv, alpha):
        acc_ref[h] = alpha * acc_ref[h] + pv

    def item(j, n):
        return jnp.minimum(j + n // DA_HEADS, n_tiles - 1), n % DA_HEADS

    def run_tiles(i, j0, tiles):
        issued = None
        pending = None
        for n in range(tiles * DA_HEADS):
            jn, hn = item(j0, n + depth)
            nxt = (i, jn, hn, scores(jn, hn))
            pv, alpha = consume(j0 + n // DA_HEADS, n % DA_HEADS)
            if issued is not None:
                drain(*issued)
            if pending is not None:
                rescale_add(*pending)
            issued, pending = nxt, (n % DA_HEADS, pv, alpha)
        drain(*issued)
        rescale_add(*pending)

    def open_query_tile(i):
        sub = lax.broadcasted_iota(jnp.int32, (DA_V_DIM, tile), 0)
        first_map = (sub < DA_HEAD_DIM).astype(F32)
        for h in range(DA_HEADS):
            q_t = qt_ref[0, head_cols[h], tile_span(i)].astype(F32)
            qz_ref[h] = jnp.concatenate([q_t * first_map, q_t * (1.0 - first_map)], axis=1).astype(BF16)
        for n in range(depth):
            drain(i, 0, n, scores(0, n))

    lp = lam_ref[...]
    lam = (jnp.exp(jnp.sum(lp[0:1] * lp[1:2], axis=-1, keepdims=True))
           - jnp.exp(jnp.sum(lp[2:3] * lp[3:4], axis=-1, keepdims=True)) + lambda_init)
    sub_w = sub_ref[...] * (1.0 - lambda_init)

    def query_tile(i, carry):
        m_ref[...] = jnp.full(m_ref.shape, NEG, F32)
        acc_ref[...] = jnp.zeros(acc_ref.shape, F32)

        n_key_tiles = i + 1

        def multi_tile_body(jj, c):
            run_tiles(i, jj * per_iter, per_iter)
            return c

        lax.fori_loop(0, n_key_tiles // per_iter, multi_tile_body, 0)
        for r in range(1, per_iter):
            @pl.when(n_key_tiles % per_iter == r)
            def _(r=r):
                run_tiles(i, n_key_tiles - r, r)

        for h in range(DA_HEADS):
            inv_l = 1.0 / acc_ref[h, DA_V_DIM:DA_V_DIM + 1, :]
            o = (acc_ref[h, 0:DA_V_DIM, :tile] * inv_l[:, :tile]
                 - acc_ref[h, 0:DA_V_DIM, tile:] * (lam * inv_l[:, tile:]))
            o = o * lax.rsqrt(jnp.mean(o * o, axis=0, keepdims=True) + EPS)
            o_ref[0, tile_span(i), head_cols[h]] = (jnp.transpose(o) * sub_w).astype(o_ref.dtype)
        open_query_tile(jnp.minimum(i + 1, n_tiles - 1))
        return carry

    open_query_tile(0)
    lax.fori_loop(0, n_tiles, query_tile, 0)


def _diff_attn(q_t, k, v_t, band, lam_params, subln, lambda_init):
    b, width, s = q_t.shape
    tile = ATTN_TILE
    assert s % tile == 0 and width == DA_HEADS * DA_V_DIM
    kern = functools.partial(_diff_attn_kernel, tile=tile, n_tiles=s // tile, lambda_init=lambda_init)
    return pl.pallas_call(
        kern,
        grid=(b,),
        in_specs=[_const_spec(lam_params.shape),
                  pl.BlockSpec((1, width, s), lambda bi: (bi, 0, 0)),
                  pl.BlockSpec((s, width), lambda bi: (bi, 0)),
                  pl.BlockSpec((1, DA_HEADS * DA_VT_ROWS, s), lambda bi: (bi, 0, 0)),
                  pl.BlockSpec(band.shape, lambda bi: (0, 0, 0), pipeline_mode=pl.Buffered(1)),
                  _const_spec((1, DA_V_DIM))],
        out_specs=pl.BlockSpec((1, s, width), lambda bi: (bi, 0, 0)),
        out_shape=jax.ShapeDtypeStruct((b, s, width), BF16),
        scratch_shapes=[pltpu.VMEM((DA_HEADS, DA_V_DIM, 2 * tile), BF16),
                        pltpu.VMEM((DA_HEADS, 1, 2 * tile), F32),
                        pltpu.VMEM((DA_HEADS, DA_VT_ROWS, 2 * tile), F32),
                        pltpu.VMEM((ATTN_RING, tile, 2 * tile), F32),
                        pltpu.VMEM((ATTN_RING, F32_SUBLANES, 2 * tile), F32)],
        compiler_params=_params("parallel"),
        name="diff_attn",
    )(lam_params, q_t, k, v_t, band, subln.reshape(1, DA_V_DIM))


def _out_mlp_kernel(a_ref, h_ref, wo_ref, g_ref, w1_ref, w2_ref, gf_ref, o_ref, u_ref,
                    *, ff_chunk, final_norm):
    h1 = h_ref[...] + jnp.dot(a_ref[...], wo_ref[...], preferred_element_type=F32)
    xn = _rmsnorm_rows(h1, g_ref[...]).astype(BF16)
    d_ff = w1_ref.shape[1]
    for c in range(d_ff // ff_chunk):
        sl = slice(c * ff_chunk, (c + 1) * ff_chunk)
        u = jnp.maximum(jnp.dot(xn, w1_ref[:, sl], preferred_element_type=F32), 0.0)
        u_ref[:, sl] = (u * u).astype(BF16)
    h2 = h1 + jnp.dot(u_ref[...], w2_ref[...], preferred_element_type=F32)
    if final_norm:
        h2 = _rmsnorm_rows(h2, gf_ref[...])
    o_ref[...] = h2


def _out_mlp(a2, h2, wo, g, w1, w2, gf, final_norm):
    t, d = h2.shape
    d_ff = w1.shape[1]
    kern = functools.partial(_out_mlp_kernel, ff_chunk=MLP_FF_CHUNK, final_norm=final_norm)
    single = pl.Buffered(1)
    return pl.pallas_call(
        kern,
        grid=(t // MLP_ROW_TILE,),
        in_specs=[pl.BlockSpec((MLP_ROW_TILE, a2.shape[1]), lambda i: (i, 0)),
                  pl.BlockSpec((MLP_ROW_TILE, d), lambda i: (i, 0)),
                  pl.BlockSpec(wo.shape, lambda i: (0, 0), pipeline_mode=single),
                  _const_spec((1, d)),
                  pl.BlockSpec(w1.shape, lambda i: (0, 0), pipeline_mode=single),
                  pl.BlockSpec(w2.shape, lambda i: (0, 0), pipeline_mode=single),
                  _const_spec((1, d))],
        out_specs=pl.BlockSpec((MLP_ROW_TILE, d), lambda i: (i, 0)),
        out_shape=jax.ShapeDtypeStruct((t, d), F32),
        scratch_shapes=[pltpu.VMEM((MLP_ROW_TILE, d_ff), BF16)],
        compiler_params=_params("parallel"),
        name="out_mlp_final" if final_norm else "out_mlp",
    )(a2, h2, wo, g.reshape(1, d), w1, w2, gf.reshape(1, d))


def _log_sigmoid(x):
    return -(jnp.maximum(-x, 0.0) + jnp.log(1.0 + jnp.exp(-jnp.abs(x))))


def _mlstm_layer_kernel(x_ref, xh_ref, g_ref, w_ref, cw_ref, cb_ref, gb_ref, hn_ref, o_ref,
                        cs_ref, qt_ref, k_ref, vt_ref, og_ref, gcol_ref, grow_ref, s_ref, m_ref,
                        *, tiles_per_seq, q_scale, qk_w, v_w):
    i = pl.program_id(0)
    tm = x_ref.shape[0]
    nc = ML_PROJ_CHUNK
    L = ML_CHUNK
    seq_start = (i % tiles_per_seq) == 0

    @pl.when(seq_start)
    def _():
        s_ref[...] = jnp.zeros_like(s_ref)
        m_ref[...] = jnp.zeros_like(m_ref)

    g = g_ref[...]
    xn = _rmsnorm_rows(x_ref[...], g).astype(BF16)
    xh = _rmsnorm_rows(xh_ref[...], g).astype(BF16)
    ones = jnp.ones((ML_VT_ROWS - ML_V_DIM, tm), BF16)
    n_chunks = (qk_w + 2 * v_w) // nc
    gate_chunk = n_chunks

    def proj_matmul(c):
        if c == gate_chunk:
            return jnp.dot(xn, w_ref[:, c * nc:c * nc + ML_GATE_PAD], preferred_element_type=F32)
        return jnp.dot(xn, w_ref[:, c * nc:(c + 1) * nc], preferred_element_type=F32)

    def proj_epilogue(c, y):
        col0 = c * nc
        cols = slice(col0, col0 + nc)
        if c == gate_chunk:
            gates = y + gb_ref[...]
            lane = lax.broadcasted_iota(jnp.int32, gates.shape, 1)
            gates = jnp.where(lane < ML_HEADS, gates, _log_sigmoid(gates))
            gcol_ref[...] = gates
            grow_ref[...] = jnp.transpose(gates)[0:grow_ref.shape[0], :]
        elif col0 < qk_w:
            slot = c % 2
            halo = jnp.dot(xh, w_ref[:, cols], preferred_element_type=F32)
            cs_ref[slot, 0:HALO, :] = jnp.where(seq_start, 0.0, halo)
            cs_ref[slot, HALO:HALO + tm, :] = y
            conv = cb_ref[:, cols]
            for j in range(CONV_WIDTH):
                off = HALO - (CONV_WIDTH - 1) + j
                conv = conv + cw_ref[j:j + 1, cols] * cs_ref[slot, off:off + tm, :]
            act = conv * jax.nn.sigmoid(conv)
            if col0 < qk_w // 2:
                qt_ref[cols, :] = jnp.transpose(act * q_scale).astype(BF16)
            else:
                k_ref[:, col0 - qk_w // 2:col0 - qk_w // 2 + nc] = act.astype(BF16)
        elif col0 < qk_w + v_w:
            r0 = ((col0 - qk_w) // ML_V_DIM) * ML_VT_ROWS
            vt_ref[r0:r0 + ML_V_DIM, :] = jnp.transpose(y).astype(BF16)
            vt_ref[r0 + ML_V_DIM:r0 + ML_VT_ROWS, :] = ones
        else:
            o0 = col0 - qk_w - v_w
            og_ref[:, o0:o0 + nc] = jax.nn.sigmoid(y).astype(BF16)

    row = lax.broadcasted_iota(jnp.int32, (L, L), 0)
    col = lax.broadcasted_iota(jnp.int32, (L, L), 1)
    key_le_query = row <= col

    def qk_blocks(c, h):
        pos = slice(c * L, (c + 1) * L)
        return (qt_ref[h * ML_QK_DIM:(h + 1) * ML_QK_DIM, pos],
                k_ref[pos, h * ML_QK_DIM:(h + 1) * ML_QK_DIM])

    def rec_matmuls(c, h):
        q_t, kh = qk_blocks(c, h)
        s_t = jnp.dot(kh, q_t, preferred_element_type=F32)
        sq = jnp.dot(s_ref[h].astype(BF16), q_t, preferred_element_type=F32)
        return s_t, sq

    def rec_head(c, h, s_t, sq):
        pos = slice(c * L, (c + 1) * L)
        _, kh = qk_blocks(c, h)
        vcols = slice(h * ML_V_DIM, (h + 1) * ML_V_DIM)
        v_t = vt_ref[h * ML_VT_ROWS:(h + 1) * ML_VT_ROWS, pos]
        ig_r = grow_ref[h:h + 1, pos]
        lf_r = grow_ref[ML_HEADS + h:ML_HEADS + h + 1, pos]
        ig_c = gcol_ref[pos, h:h + 1]
        lf_c = gcol_ref[pos, ML_HEADS + h:ML_HEADS + h + 1]
        m_prev = m_ref[h]

        b_r = jnp.sum(jnp.where(key_le_query, lf_c, 0.0), axis=0, keepdims=True)
        b_c = jnp.sum(jnp.where(key_le_query, 0.0, lf_r), axis=1, keepdims=True) + lf_c
        b_last = jnp.sum(lf_r, axis=1, keepdims=True)

        d_t = jnp.where(key_le_query, b_r - (b_c - ig_c), NEG)
        m_inter = b_r + m_prev
        m_j = jnp.maximum(m_inter, jnp.max(d_t, axis=0, keepdims=True))
        sc_t = s_t * jnp.exp(d_t - m_j)
        inter = jnp.exp(m_inter - m_j)
        tot = jnp.dot(v_t, sc_t.astype(BF16), preferred_element_type=F32) + inter * sq
        den = tot[ML_V_DIM:ML_V_DIM + 1, :]
        h_t = tot[0:ML_V_DIM, :] * (1.0 / jnp.maximum(jnp.abs(den), jnp.exp(-m_j)))
        y_t = h_t * lax.rsqrt(jnp.mean(h_t * h_t, axis=0, keepdims=True) + EPS)
        y = jnp.transpose(y_t) * hn_ref[:, vcols]
        o_ref[pos, vcols] = (og_ref[pos, vcols].astype(F32) * y).astype(o_ref.dtype)

        a_r = b_last - b_r + ig_r
        m_new = jnp.maximum(b_last + m_prev, jnp.max(a_r, axis=1, keepdims=True))
        w_r = jnp.exp(a_r - m_new)
        decay = jnp.exp(b_last + m_prev - m_new)
        wv_t = (v_t.astype(F32) * w_r).astype(BF16)
        s_ref[h] = decay * s_ref[h] + jnp.dot(wv_t, kh, preferred_element_type=F32)
        m_ref[h] = m_new

    qk_chunks = list(range(qk_w // nc))
    v_chunks = list(range(qk_w // nc, (qk_w + v_w) // nc))
    o_chunks = list(range((qk_w + v_w) // nc, n_chunks))
    first = [gate_chunk]
    while qk_chunks or v_chunks:
        first += v_chunks[:1] + qk_chunks[:1]
        qk_chunks, v_chunks = qk_chunks[1:], v_chunks[1:]
    rec_items = [(c, h) for c in range(tm // L) for h in range(ML_HEADS)]
    assert len(o_chunks) == ML_HEADS and ML_PROJ_AHEAD < ML_HEADS
    stages = [("proj", c) for c in first]
    for h in range(ML_HEADS):
        stages += [("proj", o_chunks[h]), ("rec", rec_items[h])]
    stages += [("rec", it) for it in rec_items[ML_HEADS:]]

    def issue(stage):
        kind, arg = stage
        return proj_matmul(arg) if kind == "proj" else rec_matmuls(*arg)

    ahead = ML_PROJ_AHEAD
    n_proj = len(first)
    issued = {}
    for n in range(min(ahead, n_proj)):
        issued[n] = issue(stages[n])
    for n, stage in enumerate(stages):
        nxt = n + ahead
        if nxt < len(stages) and nxt not in issued:
            if stages[nxt][0] == "proj" or n >= n_proj - 1:
                issued[nxt] = issue(stages[nxt])
        if n not in issued:
            issued[n] = issue(stage)
        kind, arg = stage
        if kind == "proj":
            proj_epilogue(arg, issued.pop(n))
        else:
            rec_head(*arg, *issued.pop(n))


def _mlstm_layer(x2, xnorm_g, w_all, conv_w, conv_b, gate_b, head_norm, batch, seq_len):
    t, d = x2.shape
    tm = ROW_TILE
    qk_w = 2 * ML_HEADS * ML_QK_DIM
    v_w = ML_HEADS * ML_V_DIM
    assert seq_len % tm == 0 and tm % HALO == 0 and tm % ML_CHUNK == 0 and ML_PROJ_CHUNK == ML_V_DIM
    assert w_all.shape[1] == qk_w + 2 * v_w + ML_GATE_PAD
    tiles = seq_len // tm
    kern = functools.partial(_mlstm_layer_kernel, tiles_per_seq=tiles, q_scale=ML_QK_DIM ** -0.5,
                             qk_w=qk_w, v_w=v_w)
    halo_blocks = tm // HALO
    return pl.pallas_call(
        kern,
        grid=(t // tm,),
        in_specs=[pl.BlockSpec((tm, d), lambda i: (i, 0)),
                  pl.BlockSpec((HALO, d), lambda i: (jnp.maximum(i * halo_blocks - 1, 0), 0)),
                  _const_spec((1, d)),
                  pl.BlockSpec(w_all.shape, lambda i: (0, 0), pipeline_mode=pl.Buffered(1)),
                  _const_spec(conv_w.shape),
                  _const_spec((1, qk_w)),
                  _const_spec((1, ML_GATE_PAD)),
                  _const_spec((1, v_w))],
        out_specs=pl.BlockSpec((tm, v_w), lambda i: (i, 0)),
        out_shape=jax.ShapeDtypeStruct((t, v_w), BF16),
        scratch_shapes=[pltpu.VMEM((2, HALO + tm, ML_PROJ_CHUNK), F32),
                        pltpu.VMEM((qk_w // 2, tm), BF16),
                        pltpu.VMEM((tm, qk_w // 2), BF16),
                        pltpu.VMEM((ML_HEADS * ML_VT_ROWS, tm), BF16),
                        pltpu.VMEM((tm, v_w), BF16),
                        pltpu.VMEM((tm, ML_GATE_PAD), F32),
                        pltpu.VMEM((2 * ML_HEADS, tm), F32),
                        pltpu.VMEM((ML_HEADS, ML_VT_ROWS, ML_QK_DIM), F32),
                        pltpu.VMEM((ML_HEADS, 1, 1), F32)],
        compiler_params=_params("arbitrary"),
        name="mlstm_layer",
    )(x2, x2, xnorm_g.reshape(1, d), w_all, conv_w, conv_b.reshape(1, qk_w),
      gate_b.reshape(1, ML_GATE_PAD), head_norm.reshape(1, v_w))


def kernel(x, rel_bias, attn_norm, attn_w_in, attn_lambda_q1, attn_lambda_k1, attn_lambda_q2,
           attn_lambda_k2, attn_subln, attn_w_out, mlstm_norm, mlstm_w_in, mlstm_b_i, mlstm_b_f,
           mlstm_conv_w, mlstm_conv_b, mlstm_head_norm, mlstm_w_out, mlp_norm, mlp_w1, mlp_w2,
           final_norm):
    batch, seq_len, d = x.shape
    t = batch * seq_len
    h = x.reshape(t, d)

    lambda_init = 0.8 - 0.6 * math.exp(-0.3 * 0)
    q_t, k, v_t = _attn_proj(h, attn_norm[0], attn_w_in[0].astype(BF16), batch, seq_len)
    band = _bias_band(rel_bias, ATTN_TILE)
    lam_params = jnp.stack([attn_lambda_q1[0], attn_lambda_k1[0],
                            attn_lambda_q2[0], attn_lambda_k2[0]]).astype(F32)
    attn = _diff_attn(q_t, k, v_t, band, lam_params, attn_subln[0], lambda_init)
    h = _out_mlp(attn.reshape(t, -1), h, attn_w_out[0].astype(BF16), mlp_norm[0],
                 mlp_w1[0].astype(BF16), mlp_w2[0].astype(BF16), final_norm, final_norm=False)

    n_gates = 2 * ML_HEADS
    w_all = jnp.pad(mlstm_w_in[0], ((0, 0), (0, ML_GATE_PAD - n_gates))).astype(BF16)
    gate_b = jnp.pad(jnp.concatenate([mlstm_b_i[0], mlstm_b_f[0]]).astype(F32),
                     (0, ML_GATE_PAD - n_gates))
    mix = _mlstm_layer(h, mlstm_norm[0], w_all, mlstm_conv_w[0], mlstm_conv_b[0], gate_b,
                       mlstm_head_norm[0], batch, seq_len)
    out = _out_mlp(mix, h, mlstm_w_out[0].astype(BF16), mlp_norm[1],
                   mlp_w1[1].astype(BF16), mlp_w2[1].astype(BF16), final_norm, final_norm=True)
    return out.reshape(batch, seq_len, d)
```
